```python
import math, functools
import jax, jax.numpy as jnp
from jax import lax
import numpy as np

D_MODEL = 4096
BATCH = 4
SEQ = 2048
DEPTH = 2
DEC_BATCH = 8
DEC_SEQ = 8
PAST_LEN = 16384
PAGE_SIZE = 128

POOL_WINDOWS = (2, 4, 8, 16)
N_POOL_GROUPS = 4
D_A = D_MODEL // 4
G_A = D_A // N_POOL_GROUPS
POOL_BUF = 15
D_B = D_MODEL // 4
DK_B = 128
H_B = D_B // DK_B
DV_B = D_B // H_B
HGRN_CHUNK = 64
D_C = D_MODEL // 2
DH_C = 128
H_C = D_C // DH_C
Q_BLOCK = 128
FORGET_BIAS_INIT = 3.0
D_FF = 4 * D_MODEL
D_PLE = 256
DN_ALPHA = (2 * DEPTH) ** 0.25
DN_BETA = (8 * DEPTH) ** -0.25
LN_EPS = 1e-5
RMS_EPS = 1e-6
IN_SPLITS = (D_A, D_B, D_B, D_B, D_B, D_C, D_C, D_C, H_C, D_MODEL, D_MODEL, D_MODEL)
D_IN = D_A + 4 * D_B + 3 * D_C + H_C + 3 * D_MODEL

kernel_name = "hybrid_pool_hgrn2_fox_decoder_step"


def _layer_norm(x, g, b):
    xf = x.astype(jnp.float32)
    mu = xf.mean(-1, keepdims=True)
    var = jnp.square(xf - mu).mean(-1, keepdims=True)
    return ((xf - mu) * lax.rsqrt(var + LN_EPS) * g.astype(jnp.float32) + b.astype(jnp.float32)).astype(x.dtype)


def _pool_mix(u, buf, pos0, w_grp, scale):
    T = u.shape[1]
    ext = jnp.concatenate([buf.astype(u.dtype), u], axis=1)
    extf = ext.astype(jnp.float32)
    cs = jnp.concatenate([jnp.zeros_like(extf[:, :1]), jnp.cumsum(extf, axis=1)], axis=1)
    end = cs[:, POOL_BUF + 1:]
    pos = pos0 + jnp.arange(T)
    outs = []
    for gi, w in enumerate(POOL_WINDOWS):
        sl = slice(gi * G_A, (gi + 1) * G_A)
        win = end[..., sl] - cs[:, POOL_BUF + 1 - w:POOL_BUF + 1 - w + T, sl]
        cnt = jnp.minimum(pos + 1, w).astype(jnp.float32)[None, :, None]
        outs.append(win / cnt)
    pooled = jnp.concatenate(outs, axis=-1) - u.astype(jnp.float32)
    B = u.shape[0]
    y = jnp.einsum('btgc,gcd->btgd', pooled.reshape(B, T, N_POOL_GROUPS, G_A), w_grp.astype(jnp.float32))
    y = y.reshape(B, T, D_A) * scale.astype(jnp.float32)
    return y.astype(u.dtype), ext[:, -POOL_BUF:]


def _hgrn2(q, f_raw, i, g, lb, s0, norm_g):
    B, T, _ = q.shape
    f32 = jnp.float32
    z = f_raw.astype(f32)
    lbf = lb.astype(f32)
    log_f = jnp.logaddexp(jnp.log(lbf), jnp.log1p(-lbf) + jax.nn.log_sigmoid(z))
    k = (1.0 - lbf) * jax.nn.sigmoid(-z)
    C = math.gcd(T, HGRN_CHUNK)
    N = T // C

    def heads(a, d):
        return a.reshape(B, N, C, H_B, d).transpose(1, 0, 3, 2, 4)

    qh = heads(q.astype(f32) * DK_B ** -0.5, DK_B)
    kh = heads(k, DK_B)
    vh = heads(i.astype(f32), DV_B)
    lfh = heads(log_f, DK_B)
    causal = jnp.tril(jnp.ones((C, C), dtype=bool))[:, :, None]

    def step(S, inp):
        qc, kc, vc, lfc = inp
        b = jnp.cumsum(lfc, axis=2)
        o_inter = jnp.einsum('bhtk,bhkv->bhtv', qc * jnp.exp(b), S)
        dec = jnp.where(causal, b[:, :, :, None, :] - b[:, :, None, :, :], -jnp.inf)
        att = jnp.einsum('bhtk,bhsk,bhtsk->bhts', qc, kc, jnp.exp(dec))
        o = o_inter + jnp.einsum('bhts,bhsv->bhtv', att, vc)
        b_last = b[:, :, -1]
        k_dec = kc * jnp.exp(b_last[:, :, None, :] - b)
        S_new = jnp.exp(b_last)[..., None] * S + jnp.einsum('bhsk,bhsv->bhkv', k_dec, vc)
        return S_new, o

    S_fin, o = lax.scan(step, s0.astype(f32), (qh, kh, vh, lfh))
    o = o.transpose(1, 0, 3, 2, 4).reshape(B, T, H_B, DV_B)
    o = o * lax.rsqrt(jnp.mean(jnp.square(o), axis=-1, keepdims=True) + RMS_EPS) * norm_g.astype(f32)
    o = o.reshape(B, T, D_B) * jax.nn.sigmoid(g.astype(f32))
    return o.astype(q.dtype), S_fin


def _fox_block(q_blk, cq_blk, qpos, k, v, ck, kpos):
    f32 = jnp.float32
    s = jnp.einsum('bqhd,bkhd->bhqk', q_blk.astype(f32), k.astype(f32)) * DH_C ** -0.5
    bias = jnp.transpose(cq_blk, (0, 2, 1))[..., :, None] - jnp.transpose(ck, (0, 2, 1))[..., None, :]
    mask = kpos[None, :] <= qpos[:, None]
    p = jax.nn.softmax(jnp.where(mask, s + bias, -jnp.inf), axis=-1)
    return jnp.einsum('bhqk,bkhd->bqhd', p, v.astype(f32))


def _fox_prompt(q, k, v, logf):
    B, S = q.shape[:2]
    c = jnp.cumsum(logf, axis=1)
    pos = jnp.arange(S)
    qb_len = math.gcd(S, Q_BLOCK)
    nb = S // qb_len
    qb = q.reshape(B, nb, qb_len, H_C, DH_C).transpose(1, 0, 2, 3, 4)
    cb = c.reshape(B, nb, qb_len, H_C).transpose(1, 0, 2, 3)
    pb = pos.reshape(nb, qb_len)
    out = lax.map(lambda a: _fox_block(a[0], a[1], a[2], k, v, c, pos), (qb, cb, pb))
    return out.transpose(1, 0, 2, 3, 4).reshape(B, S, D_C).astype(q.dtype)


def _fox_sample(q, k, v, logf, cache_k, cache_v, cache_logf, page_table):
    B, T = q.shape[:2]
    P = page_table.shape[1] * PAGE_SIZE
    kp = cache_k[page_table].reshape(B, P, H_C, DH_C)
    vp = cache_v[page_table].reshape(B, P, H_C, DH_C)
    lfp = cache_logf[page_table].reshape(B, P, H_C).astype(jnp.float32)
    suffix = lax.cumsum(lfp, axis=1, reverse=True) - lfp
    c_new = jnp.cumsum(logf, axis=1)
    k_all = jnp.concatenate([kp, k.astype(kp.dtype)], axis=1)
    v_all = jnp.concatenate([vp, v.astype(vp.dtype)], axis=1)
    c_k = jnp.concatenate([-suffix, c_new], axis=1)
    kpos = jnp.arange(P + T)
    qpos = P + jnp.arange(T)
    return _fox_block(q, c_new, qpos, k_all, v_all, c_k, kpos).reshape(B, T, D_C).astype(q.dtype)


def _layer(x, p_emb, pool_buf, pos0, hgrn_s0, lb, fox_attend, lw):
    B, T, _ = x.shape
    z = jnp.einsum('btd,de->bte', x, lw['w_in'])
    offs = np.cumsum(IN_SPLITS)[:-1].tolist()
    u_a, q_b, f_b, i_b, g_b, q_c, k_c, v_c, f_c, gate_a, gate_b, gate_c = jnp.split(z, offs, axis=-1)
    y_a, new_buf = _pool_mix(u_a, pool_buf, pos0, lw['w_pool'], lw['pool_scale'])
    y_b, new_s = _hgrn2(q_b, f_b, i_b, g_b, lb, hgrn_s0, lw['hgrn_norm_g'])
    logf_c = jax.nn.log_sigmoid((f_c + lw['fox_fb']).astype(jnp.float32))
    qh = q_c.reshape(B, T, H_C, DH_C)
    kh = k_c.reshape(B, T, H_C, DH_C)
    vh = v_c.reshape(B, T, H_C, DH_C)
    y_c = fox_attend(qh, kh, vh, logf_c)
    merged = (jax.nn.sigmoid(gate_a) * jnp.einsum('btc,cd->btd', y_a, lw['w_up_a'])
              + jax.nn.sigmoid(gate_b) * jnp.einsum('btc,cd->btd', y_b, lw['w_up_b'])
              + jax.nn.sigmoid(gate_c) * jnp.einsum('btc,cd->btd', y_c, lw['w_up_c']))
    mix = jnp.einsum('btd,de->bte', merged, lw['w_out'])
    h = _layer_norm(DN_ALPHA * x + mix, lw['ln1_g'], lw['ln1_b'])
    ff = jnp.einsum('btf,fd->btd', jnp.square(jax.nn.relu(jnp.einsum('btd,df->btf', h, lw['w_ff_up']))), lw['w_ff_down'])
    ple = jnp.einsum('btp,pd->btd', p_emb, lw['w_ple']) * jax.nn.sigmoid(jnp.einsum('btd,de->bte', h, lw['w_ple_gate']))
    x_new = _layer_norm(DN_ALPHA * h + ff + ple, lw['ln2_g'], lw['ln2_b'])
    return x_new, new_buf, new_s.astype(x.dtype), kh, vh, logf_c.astype(x.dtype)


def setup_inputs(seed: int = 0) -> dict:
    key = jax.random.key(seed)
    ks = jax.random.split(key, 32)
    f32 = jnp.float32
    n_pages = PAST_LEN // PAGE_SIZE
    n_used = DEC_BATCH * n_pages
    n_pool = n_used + max(1, n_used // 4)
    nrm = lambda k, shape, s: jax.random.normal(k, shape, f32) * s
    page_table = jax.random.permutation(ks[0], n_pool)[:n_used].reshape(DEC_BATCH, n_pages).astype(jnp.int32)
    return {
        "x_prompt": nrm(ks[1], (BATCH, SEQ, D_MODEL), 1.0),
        "x_sample": nrm(ks[2], (DEC_BATCH, DEC_SEQ, D_MODEL), 1.0),
        "cache_k": nrm(ks[3], (DEPTH, n_pool, PAGE_SIZE, H_C, DH_C), 1.0),
        "cache_v": nrm(ks[4], (DEPTH, n_pool, PAGE_SIZE, H_C, DH_C), 1.0),
        "cache_logf": jax.nn.log_sigmoid(FORGET_BIAS_INIT + nrm(ks[5], (DEPTH, n_pool, PAGE_SIZE, H_C), 1.0)),
        "state_hgrn": nrm(ks[6], (DEPTH, DEC_BATCH, H_B, DK_B, DV_B), 0.5),
        "state_pool": nrm(ks[7], (DEPTH, DEC_BATCH, POOL_BUF, D_A), 1.0),
        "page_table": page_table,
        "p_prompt": nrm(ks[8], (DEPTH, BATCH, SEQ, D_PLE), 1.0),
        "p_sample": nrm(ks[9], (DEPTH, DEC_BATCH, DEC_SEQ, D_PLE), 1.0),
        "w_in": nrm(ks[10], (DEPTH, D_MODEL, D_IN), D_MODEL ** -0.5),
        "fox_fb": FORGET_BIAS_INIT + nrm(ks[11], (DEPTH, H_C), 0.1),
        "w_pool": nrm(ks[12], (DEPTH, N_POOL_GROUPS, G_A, G_A), G_A ** -0.5),
        "pool_scale": 1.0 + nrm(ks[13], (DEPTH, D_A), 0.02),
        "hgrn_lb": nrm(ks[14], (DEPTH, D_B), 1.0),
        "hgrn_norm_g": 1.0 + nrm(ks[15], (DEPTH, DV_B), 0.02),
        "w_up_a": nrm(ks[16], (DEPTH, D_A, D_MODEL), D_A ** -0.5),
        "w_up_b": nrm(ks[17], (DEPTH, D_B, D_MODEL), D_B ** -0.5),
        "w_up_c": nrm(ks[18], (DEPTH, D_C, D_MODEL), D_C ** -0.5),
        "w_out": nrm(ks[19], (DEPTH, D_MODEL, D_MODEL), D_MODEL ** -0.5 * DN_BETA),
        "ln1_g": 1.0 + nrm(ks[20], (DEPTH, D_MODEL), 0.02),
        "ln1_b": nrm(ks[21], (DEPTH, D_MODEL), 0.02),
        "w_ff_up": nrm(ks[22], (DEPTH, D_MODEL, D_FF), D_MODEL ** -0.5),
        "w_ff_down": nrm(ks[23], (DEPTH, D_FF, D_MODEL), D_FF ** -0.5 * DN_BETA),
        "w_ple": nrm(ks[24], (DEPTH, D_PLE, D_MODEL), D_PLE ** -0.5 * DN_BETA),
        "w_ple_gate": nrm(ks[25], (DEPTH, D_MODEL, D_MODEL), D_MODEL ** -0.5),
        "ln2_g": 1.0 + nrm(ks[26], (DEPTH, D_MODEL), 0.02),
        "ln2_b": nrm(ks[27], (DEPTH, D_MODEL), 0.02),
    }


def reference(x_prompt, x_sample, cache_k, cache_v, cache_logf, state_hgrn, state_pool, page_table, p_prompt, p_sample,
              w_in, fox_fb, w_pool, pool_scale, hgrn_lb, hgrn_norm_g, w_up_a, w_up_b, w_up_c, w_out, ln1_g, ln1_b,
              w_ff_up, w_ff_down, w_ple, w_ple_gate, ln2_g, ln2_b):
    past_len = page_table.shape[1] * PAGE_SIZE
    lb_cum = jnp.cumsum(jax.nn.softmax(hgrn_lb.astype(jnp.float32), axis=0), axis=0)
    lower_bounds = lb_cum - lb_cum[:1]
    xp, xs = x_prompt, x_sample
    Bp, Bs = x_prompt.shape[0], x_sample.shape[0]
    kp_l, vp_l, fp_l, ks_l, vs_l, fs_l, sp_l, ss_l, bp_l, bs_l = ([] for _ in range(10))
    for i in range(DEPTH):
        lw = {"w_in": w_in[i], "fox_fb": fox_fb[i], "w_pool": w_pool[i], "pool_scale": pool_scale[i],
              "hgrn_norm_g": hgrn_norm_g[i], "w_up_a": w_up_a[i], "w_up_b": w_up_b[i], "w_up_c": w_up_c[i],
              "w_out": w_out[i], "ln1_g": ln1_g[i], "ln1_b": ln1_b[i], "w_ff_up": w_ff_up[i],
              "w_ff_down": w_ff_down[i], "w_ple": w_ple[i], "w_ple_gate": w_ple_gate[i],
              "ln2_g": ln2_g[i], "ln2_b": ln2_b[i]}
        xp, buf_p, s_p, k_p, v_p, f_p = _layer(
            xp, p_prompt[i], jnp.zeros((Bp, POOL_BUF, D_A), xp.dtype), 0,
            jnp.zeros((Bp, H_B, DK_B, DV_B), jnp.float32), lower_bounds[i], _fox_prompt, lw)
        sample_attend = functools.partial(_fox_sample, cache_k=cache_k[i], cache_v=cache_v[i],
                                          cache_logf=cache_logf[i], page_table=page_table)
        xs, buf_s, s_s, k_s, v_s, f_s = _layer(
            xs, p_sample[i], state_pool[i], past_len, state_hgrn[i], lower_bounds[i], sample_attend, lw)
        kp_l.append(k_p); vp_l.append(v_p); fp_l.append(f_p)
        ks_l.append(k_s); vs_l.append(v_s); fs_l.append(f_s)
        sp_l.append(s_p); ss_l.append(s_s); bp_l.append(buf_p); bs_l.append(buf_s)
    return (xp, xs,
            jnp.stack(kp_l), jnp.stack(vp_l), jnp.stack(fp_l),
            jnp.stack(ks_l), jnp.stack(vs_l), jnp.stack(fs_l),
            jnp.stack(sp_l), jnp.stack(ss_l),
            jnp.stack(bp_l), jnp.stack(bs_l))
```

```python
import functools
import math

import jax
import jax.numpy as jnp
from jax import lax
from jax.experimental import pallas as pl
from jax.experimental.pallas import tpu as pltpu

F32 = jnp.float32
BF16 = jnp.bfloat16

D_MODEL = 4096
DEPTH = 2
PAGE_SIZE = 128
POOL_WINDOWS = (2, 4, 8, 16)
D_A = D_MODEL // 4
G_A = D_A // len(POOL_WINDOWS)
POOL_BUF = 15
D_B = D_MODEL // 4
DK_B = 128
H_B = D_B // DK_B
D_C = D_MODEL // 2
DH_C = 128
H_C = D_C // DH_C
D_FF = 4 * D_MODEL
DN_ALPHA = (2 * DEPTH) ** 0.25
LN_EPS = 1e-5
RMS_EPS = 1e-6

LANES = 128
SUBLANES = 8
VMEM_LIMIT = 56 * 1024 * 1024

HGRN_CHUNK = 16
POOL_HALO = 16

NEG_INF = float("-inf")


def _cparams(sem):
    return pltpu.CompilerParams(dimension_semantics=sem, vmem_limit_bytes=VMEM_LIMIT)


def _dot(a, b):
    return jnp.dot(a, b, preferred_element_type=F32)


def _dot_nt(a, b):
    return lax.dot_general(a, b, (((1,), (1,)), ((), ())), preferred_element_type=F32)


def _dot_tn(a, b):
    return lax.dot_general(a, b, (((0,), (0,)), ((), ())), preferred_element_type=F32)


def _dot01(sel, x):
    hi = x.astype(BF16)
    r1 = x - hi.astype(F32)
    mid = r1.astype(BF16)
    lo = (r1 - mid.astype(F32)).astype(BF16)
    return _dot(sel, hi) + _dot(sel, mid) + _dot(sel, lo)


def _log_sigmoid(z):
    return jnp.minimum(z, 0.0) - jnp.log1p(jnp.exp(-jnp.abs(z)))


def _sigmoid(z):
    return 1.0 / (1.0 + jnp.exp(-z))


def _mm_kernel(*refs, act, nk, has_side):
    if has_side:
        x_ref, w_ref, x2_ref, w2_ref, o_ref = refs
    else:
        x_ref, w_ref, o_ref = refs

    def finish(acc):
        if act == "sigmoid":
            acc = _sigmoid(acc)
        elif act == "relu2":
            acc = jnp.square(jnp.maximum(acc, 0.0))
        if has_side:
            acc = _dot(x2_ref[...], w2_ref[...]) * acc
        return acc.astype(o_ref.dtype)

    if nk == 1:
        o_ref[...] = finish(_dot(x_ref[...], w_ref[...]))
    else:
        k = pl.program_id(2)

        @pl.when(k == 0)
        def _():
            o_ref[...] = jnp.zeros_like(o_ref)

        o_ref[...] += _dot(x_ref[...], w_ref[...])

        @pl.when(k == nk - 1)
        def _():
            o_ref[...] = finish(o_ref[...])


def _mm(x, w, *, out_dtype=F32, act=None, side=None, tm=1024, tn=1024, tk=4096, n_off=0, n_out=None):
    M, K = x.shape
    n_out = w.shape[1] - n_off if n_out is None else n_out
    tm, tn, tk = min(tm, M), min(tn, n_out), min(tk, K)
    assert M % tm == 0 and n_out % tn == 0 and K % tk == 0 and n_off % tn == 0
    nk = K // tk
    assert nk == 1 or out_dtype == F32
    joff = n_off // tn
    in_specs = [pl.BlockSpec((tm, tk), lambda i, j, k: (i, k)),
                pl.BlockSpec((tk, tn), lambda i, j, k: (k, j + joff))]
    args = [x, w]
    if side is not None:
        x2, w2 = side
        K2 = x2.shape[1]
        in_specs += [pl.BlockSpec((tm, K2), lambda i, j, k: (i, 0)),
                     pl.BlockSpec((K2, tn), lambda i, j, k: (0, j))]
        args += [x2, w2]
    return pl.pallas_call(
        functools.partial(_mm_kernel, act=act, nk=nk, has_side=side is not None),
        grid=(M // tm, n_out // tn, nk),
        in_specs=in_specs,
        out_specs=pl.BlockSpec((tm, tn), lambda i, j, k: (i, j)),
        out_shape=jax.ShapeDtypeStruct((M, n_out), out_dtype),
        compiler_params=_cparams(("parallel", "parallel", "arbitrary")),
        name="dense",
    )(*args)


def _pool_kernel(u_ref, buf_ref, w_ref, scale_ref, y_ref, nbuf_ref, ext_ref, *, tt, pos0):
    t = pl.program_id(1)

    @pl.when(t == 0)
    def _():
        ext_ref[0:POOL_HALO, :] = buf_ref[0]

    u = u_ref[0]
    ext_ref[POOL_HALO:POOL_HALO + tt, :] = u
    pos = pos0 + t * tt + lax.broadcasted_iota(jnp.int32, (tt, 1), 0)
    for gi, w in enumerate(POOL_WINDOWS):
        cols = slice(gi * G_A, (gi + 1) * G_A)
        win = u[:, cols]
        for j in range(1, w):
            win = win + ext_ref[POOL_HALO - j:POOL_HALO - j + tt, cols]
        cnt = jnp.minimum(pos + 1, w).astype(F32)
        pooled = win / cnt - u[:, cols]
        y = _dot(pooled.astype(BF16), w_ref[gi]) * scale_ref[:, cols]
        y_ref[0, :, cols] = y.astype(y_ref.dtype)
    tail = ext_ref[tt:tt + POOL_HALO, :]
    nbuf_ref[0] = tail
    ext_ref[0:POOL_HALO, :] = tail


def _pool_mix(u, buf16, w_pool, scale, *, pos0, tt):
    B, T, _ = u.shape
    tt = min(tt, T)
    assert T % tt == 0
    return pl.pallas_call(
        functools.partial(_pool_kernel, tt=tt, pos0=pos0),
        grid=(B, T // tt),
        in_specs=[pl.BlockSpec((1, tt, D_A), lambda b, t: (b, t, 0)),
                  pl.BlockSpec((1, POOL_HALO, D_A), lambda b, t: (b, 0, 0)),
                  pl.BlockSpec((len(POOL_WINDOWS), G_A, G_A), lambda b, t: (0, 0, 0)),
                  pl.BlockSpec((1, D_A), lambda b, t: (0, 0))],
        out_specs=[pl.BlockSpec((1, tt, D_A), lambda b, t: (b, t, 0)),
                   pl.BlockSpec((1, POOL_HALO, D_A), lambda b, t: (b, 0, 0))],
        out_shape=[jax.ShapeDtypeStruct((B, T, D_A), BF16),
                   jax.ShapeDtypeStruct((B, POOL_HALO, D_A), F32)],
        scratch_shapes=[pltpu.VMEM((POOL_HALO + tt, D_A), F32)],
        compiler_params=_cparams(("parallel", "arbitrary")),
        name="pool_mix",
    )(u, buf16, w_pool, scale)


def _hgrn_kernel(q_ref, f_ref, i_ref, g_ref, lbraw_ref, s0_ref, ng_ref, y_ref, sfin_ref,
                 st_ref, b_ref, qs_ref, kk_ref, *, layer, tt, t_valid):
    t = pl.program_id(1)
    nt = pl.num_programs(1)
    C = HGRN_CHUNK

    @pl.when(t == 0)
    def _():
        for h in range(H_B):
            st_ref[h] = s0_ref[0, h].T

    raw = lbraw_ref[...]
    e = jnp.exp(raw - jnp.max(raw, axis=0, keepdims=True))
    sm = e / jnp.sum(e, axis=0, keepdims=True)
    lb = jnp.zeros((1, D_B), F32)
    for j in range(1, layer + 1):
        lb = lb + sm[j:j + 1]
    lb = (lb + sm[0:1]) - sm[0:1]

    z = f_ref[0]
    la = jnp.log(lb)
    c = jnp.log1p(-lb) + _log_sigmoid(z)
    amax = jnp.maximum(la, c)
    delta = la - c
    lf = jnp.where(jnp.isnan(delta), la + c, amax + jnp.log1p(jnp.exp(-jnp.abs(delta))))
    kk = (1.0 - lb) * _sigmoid(-z)
    if t_valid is not None:
        row = t * tt + lax.broadcasted_iota(jnp.int32, (tt, 1), 0)
        lf = jnp.where(row < t_valid, lf, 0.0)
        kk = jnp.where(row < t_valid, kk, 0.0)
    r = lax.broadcasted_iota(jnp.int32, (tt, tt), 0)
    s = lax.broadcasted_iota(jnp.int32, (tt, tt), 1)
    tri = jnp.where((s <= r) & (s >= jnp.bitwise_and(r, -C)), 1.0, 0.0).astype(BF16)
    b_ref[...] = _dot01(tri, lf)
    kk_ref[...] = kk
    qs_ref[...] = q_ref[0] * DK_B ** -0.5

    rowc = lax.broadcasted_iota(jnp.int32, (C, 1), 0)

    def chunk(ci, carry):
        rows = pl.ds(pl.multiple_of(ci * C, C), C)
        for h in range(H_B):
            hl = slice(h * DK_B, (h + 1) * DK_B)
            b = b_ref[rows, hl]
            qs = qs_ref[rows, hl]
            kc = kk_ref[rows, hl]
            v = i_ref[0, rows, hl]
            st = st_ref[h]
            o = _dot_nt((qs * jnp.exp(b)).astype(BF16), st.astype(BF16))
            for si in range(C):
                bs = b[si:si + 1]
                d = qs * jnp.exp(jnp.minimum(b - bs, 0.0)) * kc[si:si + 1]
                rs = jnp.sum(d, axis=-1, keepdims=True)
                o = o + jnp.where(rowc >= si, rs, 0.0) * v[si:si + 1]
            b_last = b[C - 1:C]
            k_dec = kc * jnp.exp(b_last - b)
            st_ref[h] = st * jnp.exp(b_last) + _dot_tn(v.astype(BF16), k_dec.astype(BF16))
            o = o * lax.rsqrt(jnp.mean(jnp.square(o), axis=-1, keepdims=True) + RMS_EPS) * ng_ref[...]
            o = o * _sigmoid(g_ref[0, rows, hl])
            y_ref[0, rows, hl] = o.astype(y_ref.dtype)
        return carry

    lax.fori_loop(0, tt // C, chunk, 0)

    @pl.when(t == nt - 1)
    def _():
        for h in range(H_B):
            sfin_ref[0, h] = st_ref[h].T


def _hgrn2(zb, lb_raw, s0, norm_g, *, layer, tt, t_valid=None):
    B, T, _ = zb.shape
    tt = min(tt, T)
    assert T % tt == 0 and tt % HGRN_CHUNK == 0

    def col(j):
        return pl.BlockSpec((1, tt, D_B), lambda b, t: (b, t, j))

    return pl.pallas_call(
        functools.partial(_hgrn_kernel, layer=layer, tt=tt, t_valid=t_valid),
        grid=(B, T // tt),
        in_specs=[col(0), col(1), col(2), col(3),
                  pl.BlockSpec((DEPTH, D_B), lambda b, t: (0, 0)),
                  pl.BlockSpec((1, H_B, DK_B, DK_B), lambda b, t: (b, 0, 0, 0)),
                  pl.BlockSpec((1, DK_B), lambda b, t: (0, 0))],
        out_specs=[pl.BlockSpec((1, tt, D_B), lambda b, t: (b, t, 0)),
                   pl.BlockSpec((1, H_B, DK_B, DK_B), lambda b, t: (b, 0, 0, 0))],
        out_shape=[jax.ShapeDtypeStruct((B, T, D_B), BF16),
                   jax.ShapeDtypeStruct((B, H_B, DK_B, DK_B), F32)],
        scratch_shapes=[pltpu.VMEM((H_B, DK_B, DK_B), F32),
                        pltpu.VMEM((tt, D_B), F32), pltpu.VMEM((tt, D_B), F32), pltpu.VMEM((tt, D_B), F32)],
        compiler_params=_cparams(("parallel", "arbitrary")),
        name="hgrn2",
    )(zb, zb, zb, zb, lb_raw, s0, norm_g)


def _fox_gate_kernel(f_ref, fb_ref, logf_ref, c_ref, *, t_valid):
    S = f_ref.shape[1]
    logf = _log_sigmoid(f_ref[0] + fb_ref[...])
    logf_ref[0] = logf
    if t_valid is not None:
        row = lax.broadcasted_iota(jnp.int32, (S, 1), 0)
        logf = jnp.where(row < t_valid, logf, 0.0)
    r = lax.broadcasted_iota(jnp.int32, (S, S), 0)
    s = lax.broadcasted_iota(jnp.int32, (S, S), 1)
    tri = jnp.where(s <= r, 1.0, 0.0).astype(BF16)
    c_ref[0] = _dot01(tri, logf)


def _fox_gate(f_pad, fb_pad, *, t_valid=None):
    B, S, L = f_pad.shape
    spec = pl.BlockSpec((1, S, L), lambda b: (b, 0, 0))
    return pl.pallas_call(
        functools.partial(_fox_gate_kernel, t_valid=t_valid),
        grid=(B,),
        in_specs=[spec, pl.BlockSpec((1, L), lambda b: (0, 0))],
        out_specs=[spec, spec],
        out_shape=[jax.ShapeDtypeStruct((B, S, L), F32)] * 2,
        compiler_params=_cparams(("parallel",)),
        name="fox_gate",
    )(f_pad, fb_pad)


def _fox_prompt_kernel(q_ref, k_ref, v_ref, ccol_ref, crow_ref, o_ref, *, tq):
    h = pl.program_id(1)
    qi = pl.program_id(2)
    q = q_ref[0]
    lane = lax.broadcasted_iota(jnp.int32, (tq, LANES), 1)
    cq = jnp.sum(jnp.where(lane == h, ccol_ref[0], 0.0), axis=-1, keepdims=True)
    scale = DH_C ** -0.5

    def block(j, carry, masked):
        m, l, acc = carry
        ks = pl.ds(pl.multiple_of(j * tq, tq), tq)
        kb = k_ref[0, ks, :].astype(BF16)
        vb = v_ref[0, ks, :].astype(BF16)
        s = _dot_nt(q, kb) * scale
        s = s + (cq - crow_ref[0, 0, :, ks])
        if masked:
            rr = lax.broadcasted_iota(jnp.int32, (tq, tq), 0)
            cc = lax.broadcasted_iota(jnp.int32, (tq, tq), 1)
            s = jnp.where(cc <= rr, s, NEG_INF)
        m_new = jnp.maximum(m, jnp.max(s, axis=-1, keepdims=True))
        alpha = jnp.exp(m - m_new)
        p = jnp.exp(s - m_new)
        l = alpha * l + jnp.sum(p, axis=-1, keepdims=True)
        acc = alpha * acc + _dot(p.astype(BF16), vb)
        return m_new, l, acc

    init = (jnp.full((tq, 1), NEG_INF, F32), jnp.zeros((tq, 1), F32), jnp.zeros((tq, DH_C), F32))
    carry = block(qi, init, True)
    m, l, acc = lax.fori_loop(0, qi, lambda j, cr: block(j, cr, False), carry)
    o_ref[0] = (acc / l).astype(o_ref.dtype)


def _fox_prompt(q, k, v, c_col, c_row, *, tq=512):
    B, S, _ = q.shape
    tq = min(tq, S)
    assert S % tq == 0
    kv_spec = pl.BlockSpec((1, S, DH_C), lambda b, h, i: (b, 0, h))
    return pl.pallas_call(
        functools.partial(_fox_prompt_kernel, tq=tq),
        grid=(B, H_C, S // tq),
        in_specs=[pl.BlockSpec((1, tq, DH_C), lambda b, h, i: (b, i, h)), kv_spec, kv_spec,
                  pl.BlockSpec((1, tq, LANES), lambda b, h, i: (b, i, 0)),
                  pl.BlockSpec((1, 1, 1, S), lambda b, h, i: (b, h, 0, 0))],
        out_specs=pl.BlockSpec((1, tq, DH_C), lambda b, h, i: (b, i, h)),
        out_shape=jax.ShapeDtypeStruct((B, S, D_C), BF16),
        compiler_params=_cparams(("parallel", "parallel", "arbitrary")),
        name="fox_prompt",
    )(q, k, v, c_col, c_row)


def _fox_suffix_kernel(pt_ref, lf_ref, usel_ref, tsel_ref, esel_ref, o_ref, g_ref, *, n_pages):
    b = pl.program_id(0)
    for p in range(n_pages):
        g_ref[p:p + 1, :] = lf_ref[pl.ds(pt_ref[b, p], 1), :]
    g = g_ref[...]
    within = _dot01_rhs(g, usel_ref[...])
    tot = _dot01_rhs(g, tsel_ref[...])
    r = lax.broadcasted_iota(jnp.int32, (n_pages, n_pages), 0)
    s = lax.broadcasted_iota(jnp.int32, (n_pages, n_pages), 1)
    later = jnp.where(s > r, 1.0, 0.0).astype(BF16)
    carry = _dot01(later, tot)
    suf = within + _dot01_rhs(carry, esel_ref[...])
    for h in range(H_C):
        o_ref[0, h] = suf[:, h * PAGE_SIZE:(h + 1) * PAGE_SIZE]


def _dot01_rhs(x, sel):
    hi = x.astype(BF16)
    r1 = x - hi.astype(F32)
    mid = r1.astype(BF16)
    lo = (r1 - mid.astype(F32)).astype(BF16)
    return _dot(hi, sel) + _dot(mid, sel) + _dot(lo, sel)


def _fox_suffix(page_table, cache_logf_l):
    B, n_pages = page_table.shape
    n_pool, W = cache_logf_l.shape
    key_in = jnp.arange(W) // H_C
    head_in = jnp.arange(W) % H_C
    head_out = jnp.arange(W) // PAGE_SIZE
    key_out = jnp.arange(W) % PAGE_SIZE
    usel = ((head_in[:, None] == head_out[None, :]) & (key_in[:, None] > key_out[None, :])).astype(BF16)
    tsel = (head_in[:, None] == jnp.arange(LANES)[None, :]).astype(BF16)
    esel = (jnp.arange(LANES)[:, None] == head_out[None, :]).astype(BF16)
    full = lambda shp: pl.BlockSpec(shp, lambda b, pt: tuple(0 for _ in shp))
    grid_spec = pltpu.PrefetchScalarGridSpec(
        num_scalar_prefetch=1,
        grid=(B,),
        in_specs=[full((n_pool, W)), full((W, W)), full((W, LANES)), full((LANES, W))],
        out_specs=pl.BlockSpec((1, H_C, n_pages, PAGE_SIZE), lambda b, pt: (b, 0, 0, 0)),
        scratch_shapes=[pltpu.VMEM((n_pages, W), F32)],
    )
    return pl.pallas_call(
        functools.partial(_fox_suffix_kernel, n_pages=n_pages),
        grid_spec=grid_spec,
        out_shape=jax.ShapeDtypeStruct((B, H_C, n_pages, PAGE_SIZE), F32),
        compiler_params=_cparams(("arbitrary",)),
        name="fox_suffix",
    )(page_table, cache_logf_l, usel, tsel, esel)


def _fox_sample_kernel(pt_ref, q_ref, kn_ref, vn_ref, cq_ref, ckn_ref, suf_ref, *rest, t_new, ppg):
    k_refs = rest[:ppg]
    v_refs = rest[ppg:2 * ppg]
    o_ref, m_ref, l_ref, acc_ref = rest[2 * ppg:]
    g = pl.program_id(1)
    ng = pl.num_programs(1)
    scale = DH_C ** -0.5
    R = H_C * t_new

    def q_head(h):
        return q_ref[0, :, h * DH_C:(h + 1) * DH_C]

    def update(s, v_of_head):
        m = m_ref[...]
        m_new = jnp.maximum(m, jnp.max(s, axis=-1, keepdims=True))
        alpha = jnp.exp(m - m_new)
        p = jnp.exp(s - m_new)
        l_ref[...] = alpha * l_ref[...] + jnp.sum(p, axis=-1, keepdims=True)
        m_ref[...] = m_new
        for h in range(H_C):
            rows = slice(h * t_new, (h + 1) * t_new)
            acc_ref[rows, :] = alpha[rows] * acc_ref[rows, :] + _dot(p[rows].astype(BF16), v_of_head(h))

    @pl.when(g == 0)
    def _():
        m_ref[...] = jnp.full_like(m_ref, NEG_INF)
        l_ref[...] = jnp.zeros_like(l_ref)
        acc_ref[...] = jnp.zeros_like(acc_ref)
        s = jnp.concatenate(
            [_dot_nt(q_head(h), kn_ref[0, :, h * DH_C:(h + 1) * DH_C]) for h in range(H_C)], axis=0) * scale
        s = s + (cq_ref[0] - ckn_ref[0])
        qpos = lax.broadcasted_iota(jnp.int32, s.shape, 0) % t_new
        kpos = lax.broadcasted_iota(jnp.int32, s.shape, 1)
        s = jnp.where(kpos <= qpos, s, NEG_INF)
        update(s, lambda h: vn_ref[0, :, h * DH_C:(h + 1) * DH_C])

    for j in range(ppg):
        kj, vj = k_refs[j], v_refs[j]
        s = jnp.concatenate(
            [_dot_nt(q_head(h), kj[0, 0, :, h, :].astype(BF16)) for h in range(H_C)], axis=0) * scale
        bias = jnp.concatenate(
            [jnp.broadcast_to(suf_ref[0, h, pl.ds(g * ppg + j, 1), :], (t_new, PAGE_SIZE)) for h in range(H_C)], axis=0)
        s = s + (cq_ref[0] + bias)
        update(s, lambda h: vj[0, 0, :, h, :].astype(BF16))

    @pl.when(g == ng - 1)
    def _():
        o = acc_ref[...] / l_ref[...]
        for h in range(H_C):
            o_ref[0, :, h * DH_C:(h + 1) * DH_C] = o[h * t_new:(h + 1) * t_new].astype(o_ref.dtype)


def _fox_sample(q, k_new, v_new, cq_col, ck_new, suffix, cache_k, cache_v, page_table, *, layer, t_new, ppg=4):
    B, n_pages = page_table.shape
    T = q.shape[1]
    T16 = k_new.shape[1]
    R = H_C * t_new
    assert n_pages % ppg == 0

    def page_spec(j):
        return pl.BlockSpec((1, 1, PAGE_SIZE, H_C, DH_C),
                            lambda b, g, pt: (layer, pt[b, g * ppg + j], 0, 0, 0))

    bspec = lambda shp: pl.BlockSpec((1,) + shp, lambda b, g, pt: (b,) + tuple(0 for _ in shp))
    grid_spec = pltpu.PrefetchScalarGridSpec(
        num_scalar_prefetch=1,
        grid=(B, n_pages // ppg),
        in_specs=[bspec((T, D_C)), bspec((T16, D_C)), bspec((T16, D_C)), bspec((R, 1)), bspec((R, T16)),
                  bspec((H_C, n_pages, PAGE_SIZE))]
        + [page_spec(j) for j in range(ppg)] + [page_spec(j) for j in range(ppg)],
        out_specs=bspec((T, D_C)),
        scratch_shapes=[pltpu.VMEM((R, 1), F32), pltpu.VMEM((R, 1), F32), pltpu.VMEM((R, DH_C), F32)],
    )
    return pl.pallas_call(
        functools.partial(_fox_sample_kernel, t_new=t_new, ppg=ppg),
        grid_spec=grid_spec,
        out_shape=jax.ShapeDtypeStruct((B, T, D_C), BF16),
        compiler_params=_cparams(("parallel", "arbitrary")),
        name="fox_sample",
    )(page_table, q, k_new, v_new, cq_col, ck_new, suffix, *([cache_k] * ppg), *([cache_v] * ppg))


def _merge_kernel(ya_ref, yb_ref, yc_ref, wa_ref, wb_ref, wc_ref, ga_ref, gb_ref, gc_ref, o_ref):
    m = ga_ref[...].astype(F32) * _dot(ya_ref[...], wa_ref[...])
    m = m + gb_ref[...].astype(F32) * _dot(yb_ref[...], wb_ref[...])
    m = m + gc_ref[...].astype(F32) * _dot(yc_ref[...], wc_ref[...])
    o_ref[...] = m.astype(o_ref.dtype)


def _merge(y_a, y_b, y_c, w_a, w_b, w_c, gates, *, tm=1024, tn=512):
    M = y_a.shape[0]
    tm = min(tm, M)
    nj = D_MODEL // tn
    act = lambda K: pl.BlockSpec((tm, K), lambda i, j: (i, 0))
    wsp = lambda K: pl.BlockSpec((K, tn), lambda i, j: (0, j))
    gsp = lambda o: pl.BlockSpec((tm, tn), lambda i, j: (i, j + o * nj))
    return pl.pallas_call(
        _merge_kernel,
        grid=(M // tm, nj),
        in_specs=[act(D_A), act(D_B), act(D_C), wsp(D_A), wsp(D_B), wsp(D_C), gsp(0), gsp(1), gsp(2)],
        out_specs=pl.BlockSpec((tm, tn), lambda i, j: (i, j)),
        out_shape=jax.ShapeDtypeStruct((M, D_MODEL), BF16),
        compiler_params=_cparams(("parallel", "parallel")),
        name="merge",
    )(y_a, y_b, y_c, w_a, w_b, w_c, gates, gates, gates)


def _ln_kernel(*refs, n_delta):
    x_ref = refs[0]
    d_refs = refs[1:1 + n_delta]
    g_ref, b_ref, o_ref, ob_ref = refs[1 + n_delta:]
    y = DN_ALPHA * x_ref[...]
    for d in d_refs:
        y = y + d[...].astype(F32)
    mu = jnp.mean(y, axis=-1, keepdims=True)
    yc = y - mu
    var = jnp.mean(jnp.square(yc), axis=-1, keepdims=True)
    o = yc * lax.rsqrt(var + LN_EPS) * g_ref[...] + b_ref[...]
    o_ref[...] = o
    ob_ref[...] = o.astype(BF16)


def _res_ln(x, deltas, g, b, *, tm=128):
    M = x.shape[0]
    tm = min(tm, M)
    row = pl.BlockSpec((tm, D_MODEL), lambda i: (i, 0))
    vec = pl.BlockSpec((1, D_MODEL), lambda i: (0, 0))
    return pl.pallas_call(
        functools.partial(_ln_kernel, n_delta=len(deltas)),
        grid=(M // tm,),
        in_specs=[row] * (1 + len(deltas)) + [vec, vec],
        out_specs=[row, row],
        out_shape=[jax.ShapeDtypeStruct((M, D_MODEL), F32), jax.ShapeDtypeStruct((M, D_MODEL), BF16)],
        compiler_params=_cparams(("parallel",)),
        name="res_ln",
    )(x, *deltas, g, b)


OFF_A = 0
OFF_B = D_A
OFF_QC = OFF_B + 4 * D_B
OFF_KC = OFF_QC + D_C
OFF_VC = OFF_KC + D_C
OFF_FC = OFF_VC + D_C
OFF_GATE = OFF_FC + H_C


def _layer(x, xb, p_emb, lw, *, layer, B, T, pool_buf16, pos0, hgrn_s0, lb_raw, sample=None):
    M = B * T
    tm = min(1024, M)
    w_main = lw["w_main"]
    u_a = _mm(xb, w_main, tm=tm, n_off=OFF_A, n_out=D_A)
    zb = _mm(xb, w_main, tm=tm, n_off=OFF_B, n_out=4 * D_B)
    q_c = _mm(xb, w_main, tm=tm, n_off=OFF_QC, n_out=D_C, out_dtype=BF16)
    k_c = _mm(xb, w_main, tm=tm, n_off=OFF_KC, n_out=D_C)
    v_c = _mm(xb, w_main, tm=tm, n_off=OFF_VC, n_out=D_C)
    f_c = _mm(xb, lw["w_fc"], tm=tm, tn=LANES)
    gates = _mm(xb, lw["w_gate"], tm=tm, out_dtype=BF16, act="sigmoid")

    y_a, nbuf = _pool_mix(u_a.reshape(B, T, D_A), pool_buf16, lw["w_pool"], lw["pool_scale"], pos0=pos0, tt=256)
    new_buf = nbuf[:, 1:]

    zb3 = zb.reshape(B, T, 4 * D_B)
    if T % HGRN_CHUNK:
        Tp = -(-T // HGRN_CHUNK) * HGRN_CHUNK
        zb3 = jnp.pad(zb3, ((0, 0), (0, Tp - T), (0, 0)))
        y_b, new_s = _hgrn2(zb3, lb_raw, hgrn_s0, lw["hgrn_norm_g"], layer=layer, tt=128, t_valid=T)
        y_b = y_b[:, :T]
    else:
        y_b, new_s = _hgrn2(zb3, lb_raw, hgrn_s0, lw["hgrn_norm_g"], layer=layer, tt=128)

    q3, k3, v3 = (a.reshape(B, T, D_C) for a in (q_c, k_c, v_c))
    if sample is None:
        logf, c = _fox_gate(f_c.reshape(B, T, LANES), lw["fox_fb"])
        c_row = jnp.transpose(c[:, :, :H_C], (0, 2, 1)).reshape(B, H_C, 1, T)
        y_c = _fox_prompt(q3, k3, v3, c, c_row)
    else:
        T16 = PAGE_SIZE
        f3 = jnp.pad(f_c.reshape(B, T, LANES), ((0, 0), (0, T16 - T), (0, 0)))
        logf, c = _fox_gate(f3, lw["fox_fb"], t_valid=T)
        logf = logf[:, :T]
        c_hm = jnp.transpose(c[:, :, :H_C], (0, 2, 1))
        cq_col = c_hm[:, :, :T].reshape(B, H_C * T, 1)
        ck_new = jnp.repeat(c_hm, T, axis=1)
        pad_kv = lambda a: jnp.pad(a.astype(BF16), ((0, 0), (0, T16 - T), (0, 0)))
        suffix = _fox_suffix(sample["page_table"], sample["cache_logf"][layer])
        y_c = _fox_sample(q3, pad_kv(k3), pad_kv(v3), cq_col, ck_new, suffix, sample["cache_k"], sample["cache_v"],
                          sample["page_table"], layer=layer, t_new=T)
    logf = logf[:, :, :H_C]

    merged = _merge(y_a.reshape(M, D_A), y_b.reshape(M, D_B), y_c.reshape(M, D_C),
                    lw["w_up_a"], lw["w_up_b"], lw["w_up_c"], gates, tm=tm)
    mix = _mm(merged, lw["w_out"], tm=tm)
    h, hb = _res_ln(x, [mix], lw["ln1_g"], lw["ln1_b"])
    up = _mm(hb, lw["w_ff_up"], tm=tm, out_dtype=BF16, act="relu2")
    ff = _mm(up, lw["w_ff_down"], tm=tm)
    ple = _mm(hb, lw["w_ple_gate"], tm=tm, act="sigmoid", side=(p_emb, lw["w_ple"]))
    x_new, xb_new = _res_ln(h, [ff, ple], lw["ln2_g"], lw["ln2_b"])
    return x_new, xb_new, new_buf, new_s, k_c, v_c, logf


def kernel(x_prompt, x_sample, cache_k, cache_v, cache_logf, state_hgrn, state_pool, page_table, p_prompt, p_sample,
           w_in, fox_fb, w_pool, pool_scale, hgrn_lb, hgrn_norm_g, w_up_a, w_up_b, w_up_c, w_out, ln1_g, ln1_b,
           w_ff_up, w_ff_down, w_ple, w_ple_gate, ln2_g, ln2_b):
    Bp, Tp, _ = x_prompt.shape
    Bs, Ts, _ = x_sample.shape
    n_pool = cache_k.shape[1]
    past_len = page_table.shape[1] * PAGE_SIZE
    cache_logf2 = cache_logf.reshape(DEPTH, n_pool, PAGE_SIZE * H_C)
    sample_ctx = {"page_table": page_table, "cache_k": cache_k, "cache_v": cache_v, "cache_logf": cache_logf2}

    xp = x_prompt.reshape(Bp * Tp, D_MODEL)
    xs = x_sample.reshape(Bs * Ts, D_MODEL)
    xpb, xsb = xp.astype(BF16), xs.astype(BF16)
    zero_buf = jnp.zeros((Bp, POOL_HALO, D_A), F32)
    zero_state = jnp.zeros((Bp, H_B, DK_B, DK_B), F32)
    lb_raw = hgrn_lb.astype(F32)

    outs = [[] for _ in range(10)]
    for i in range(DEPTH):
        lw = {
            "w_main": w_in[i, :, :OFF_FC].astype(BF16),
            "w_fc": w_in[i, :, OFF_FC:OFF_FC + LANES].astype(BF16),
            "w_gate": w_in[i, :, OFF_GATE:].astype(BF16),
            "fox_fb": jnp.pad(fox_fb[i].astype(F32), (0, LANES - H_C)).reshape(1, LANES),
            "w_pool": w_pool[i].astype(BF16),
            "pool_scale": pool_scale[i].astype(F32).reshape(1, D_A),
            "hgrn_norm_g": hgrn_norm_g[i].astype(F32).reshape(1, DK_B),
            "w_up_a": w_up_a[i].astype(BF16), "w_up_b": w_up_b[i].astype(BF16), "w_up_c": w_up_c[i].astype(BF16),
            "w_out": w_out[i].astype(BF16),
            "ln1_g": ln1_g[i].reshape(1, D_MODEL), "ln1_b": ln1_b[i].reshape(1, D_MODEL),
            "w_ff_up": w_ff_up[i].astype(BF16), "w_ff_down": w_ff_down[i].astype(BF16),
            "w_ple": w_ple[i].astype(BF16), "w_ple_gate": w_ple_gate[i].astype(BF16),
            "ln2_g": ln2_g[i].reshape(1, D_MODEL), "ln2_b": ln2_b[i].reshape(1, D_MODEL),
        }
        pp = p_prompt[i].reshape(Bp * Tp, -1).astype(BF16)
        ps = p_sample[i].reshape(Bs * Ts, -1).astype(BF16)
        xp, xpb, buf_p, s_p, k_p, v_p, f_p = _layer(
            xp, xpb, pp, lw, layer=i, B=Bp, T=Tp, pool_buf16=zero_buf, pos0=0, hgrn_s0=zero_state, lb_raw=lb_raw)
        buf16 = jnp.pad(state_pool[i], ((0, 0), (POOL_HALO - POOL_BUF, 0), (0, 0)))
        xs, xsb, buf_s, s_s, k_s, v_s, f_s = _layer(
            xs, xsb, ps, lw, layer=i, B=Bs, T=Ts, pool_buf16=buf16, pos0=past_len, hgrn_s0=state_hgrn[i],
            lb_raw=lb_raw, sample=sample_ctx)
        vals = (k_p.reshape(Bp, Tp, H_C, DH_C), v_p.reshape(Bp, Tp, H_C, DH_C), f_p,
                k_s.reshape(Bs, Ts, H_C, DH_C), v_s.reshape(Bs, Ts, H_C, DH_C), f_s, s_p, s_s, buf_p, buf_s)
        for o, val in zip(outs, vals):
            o.append(val)
    return (xp.reshape(Bp, Tp, D_MODEL), xs.reshape(Bs, Ts, D_MODEL)) + tuple(jnp.stack(o) for o in outs)
```

```python
import functools
import math

import jax
import jax.numpy as jnp
from jax import lax
from jax.experimental import pallas as pl
from jax.experimental.pallas import tpu as pltpu

F32 = jnp.float32
BF16 = jnp.bfloat16

D_MODEL = 4096
DEPTH = 2
PAGE_SIZE = 128
POOL_WINDOWS = (2, 4, 8, 16)
D_A = D_MODEL // 4
G_A = D_A // len(POOL_WINDOWS)
POOL_BUF = 15
D_B = D_MODEL // 4
DK_B = 128
H_B = D_B // DK_B
D_C = D_MODEL // 2
DH_C = 128
H_C = D_C // DH_C
D_FF = 4 * D_MODEL
DN_ALPHA = (2 * DEPTH) ** 0.25
LN_EPS = 1e-5
RMS_EPS = 1e-6

LANES = 128
SUBLANES = 8
VMEM_LIMIT = 56 * 1024 * 1024

HGRN_CHUNK = 16
POOL_HALO = 16

NEG_INF = float("-inf")


def _cparams(sem):
    return pltpu.CompilerParams(dimension_semantics=sem, vmem_limit_bytes=VMEM_LIMIT)


def _dot(a, b):
    return jnp.dot(a, b, preferred_element_type=F32)


def _dot_nt(a, b):
    return lax.dot_general(a, b, (((1,), (1,)), ((), ())), preferred_element_type=F32)


def _dot_tn(a, b):
    return lax.dot_general(a, b, (((0,), (0,)), ((), ())), preferred_element_type=F32)


def _dot01(sel, x):
    hi = x.astype(BF16)
    r1 = x - hi.astype(F32)
    mid = r1.astype(BF16)
    lo = (r1 - mid.astype(F32)).astype(BF16)
    return _dot(sel, hi) + _dot(sel, mid) + _dot(sel, lo)


def _log_sigmoid(z):
    return jnp.minimum(z, 0.0) - jnp.log1p(jnp.exp(-jnp.abs(z)))


def _sigmoid(z):
    return 1.0 / (1.0 + jnp.exp(-z))


def _mm_kernel(*refs, act, nk, has_side):
    if has_side:
        x_ref, w_ref, x2_ref, w2_ref, o_ref = refs
    else:
        x_ref, w_ref, o_ref = refs

    def finish(acc):
        if act == "sigmoid":
            acc = _sigmoid(acc)
        elif act == "relu2":
            acc = jnp.square(jnp.maximum(acc, 0.0))
        if has_side:
            acc = _dot(x2_ref[...], w2_ref[...]) * acc
        return acc.astype(o_ref.dtype)

    if nk == 1:
        o_ref[...] = finish(_dot(x_ref[...], w_ref[...]))
    else:
        k = pl.program_id(2)

        @pl.when(k == 0)
        def _():
            o_ref[...] = jnp.zeros_like(o_ref)

        o_ref[...] += _dot(x_ref[...], w_ref[...])

        @pl.when(k == nk - 1)
        def _():
            o_ref[...] = finish(o_ref[...])


def _mm(x, w, *, out_dtype=F32, act=None, side=None, tm=1024, tn=1024, tk=4096, n_off=0, n_out=None):
    M, K = x.shape
    n_out = w.shape[1] - n_off if n_out is None else n_out
    tm, tn, tk = min(tm, M), min(tn, n_out), min(tk, K)
    assert M % tm == 0 and n_out % tn == 0 and K % tk == 0 and n_off % tn == 0
    nk = K // tk
    assert nk == 1 or out_dtype == F32
    joff = n_off // tn
    in_specs = [pl.BlockSpec((tm, tk), lambda i, j, k: (i, k)),
                pl.BlockSpec((tk, tn), lambda i, j, k: (k, j + joff))]
    args = [x, w]
    if side is not None:
        x2, w2 = side
        K2 = x2.shape[1]
        in_specs += [pl.BlockSpec((tm, K2), lambda i, j, k: (i, 0)),
                     pl.BlockSpec((K2, tn), lambda i, j, k: (0, j))]
        args += [x2, w2]
    return pl.pallas_call(
        functools.partial(_mm_kernel, act=act, nk=nk, has_side=side is not None),
        grid=(M // tm, n_out // tn, nk),
        in_specs=in_specs,
        out_specs=pl.BlockSpec((tm, tn), lambda i, j, k: (i, j)),
        out_shape=jax.ShapeDtypeStruct((M, n_out), out_dtype),
        compiler_params=_cparams(("parallel", "parallel", "arbitrary")),
        name="dense",
    )(*args)


PROJ_TM = 1024
PROJ_TN = 512
CAST_ROWS = 512


def _proj_kernel(*refs, act, shift, has_side):
    refs = list(refs)
    xp_ref, xs_ref, wa_ref = refs[:3]
    del refs[:3]
    wb_ref = refs.pop(0) if shift else None
    if has_side:
        pp_ref, ps_ref, w2_ref = refs[:3]
        del refs[:3]
    op_ref, os_ref, wbf_ref = refs[:3]
    w2bf_ref = refs[3] if has_side else None
    i = pl.program_id(1)
    K, tn = wbf_ref.shape

    def finish(acc, p_ref, o_ref):
        if act == "sigmoid":
            acc = _sigmoid(acc)
        elif act == "relu2":
            acc = jnp.square(jnp.maximum(acc, 0.0))
        if has_side:
            acc = _dot(p_ref[...], w2bf_ref[...]) * acc
        o_ref[...] = acc.astype(o_ref.dtype)

    @pl.when(i == 0)
    def _():
        for r in range(0, K, CAST_ROWS):
            rows = slice(r, r + CAST_ROWS)
            w = wa_ref[rows, :]
            if shift:
                wide = jnp.concatenate([w, wb_ref[rows, :]], axis=1)
                w = pltpu.roll(wide, wide.shape[1] - shift, axis=1)[:, :tn]
            wbf_ref[rows, :] = w.astype(BF16)
        if has_side:
            w2bf_ref[...] = w2_ref[...].astype(BF16)
        finish(_dot(xs_ref[...], wbf_ref[...]), ps_ref if has_side else None, os_ref)

    @pl.when(i > 0)
    def _():
        finish(_dot(xp_ref[...], wbf_ref[...]), pp_ref if has_side else None, op_ref)


def _proj(xp, xs, w, layer, *, n_off=0, n_out=None, out_dtype=F32, act=None, shift=0, side=None, tn=PROJ_TN):
    Mp, K = xp.shape
    Ms = xs.shape[0]
    n_out = w.shape[2] - n_off if n_out is None else n_out
    tm = min(PROJ_TM, Mp)
    assert Mp % tm == 0 and n_out % tn == 0 and n_off % tn == 0 and 0 <= shift < LANES and K % CAST_ROWS == 0
    nmp, joff = Mp // tm, n_off // tn
    row = lambda i: jnp.maximum(i - 1, 0)
    in_specs = [pl.BlockSpec((tm, K), lambda j, i: (row(i), 0)),
                pl.BlockSpec((Ms, K), lambda j, i: (0, 0)),
                pl.BlockSpec((None, K, tn), lambda j, i: (layer, 0, j + joff))]
    args = [xp, xs, w]
    if shift:
        per = tn // LANES
        in_specs.append(pl.BlockSpec((None, K, LANES), lambda j, i: (layer, 0, (j + joff + 1) * per)))
        args.append(w)
    scratch = [pltpu.VMEM((K, tn), BF16)]
    if side is not None:
        pp, ps, w2 = side
        K2 = pp.shape[1]
        in_specs += [pl.BlockSpec((tm, K2), lambda j, i: (row(i), 0)),
                     pl.BlockSpec((Ms, K2), lambda j, i: (0, 0)),
                     pl.BlockSpec((None, K2, tn), lambda j, i: (layer, 0, j))]
        args += [pp, ps, w2]
        scratch.append(pltpu.VMEM((K2, tn), BF16))
    return pl.pallas_call(
        functools.partial(_proj_kernel, act=act, shift=shift, has_side=side is not None),
        grid=(n_out // tn, nmp + 1),
        in_specs=in_specs,
        out_specs=[pl.BlockSpec((tm, tn), lambda j, i: (row(i), j)),
                   pl.BlockSpec((Ms, tn), lambda j, i: (0, j))],
        out_shape=[jax.ShapeDtypeStruct((Mp, n_out), out_dtype), jax.ShapeDtypeStruct((Ms, n_out), out_dtype)],
        scratch_shapes=scratch,
        compiler_params=_cparams(("parallel", "arbitrary")),
        name="proj",
    )(*args)


def _pool_kernel(u_ref, buf_ref, w_ref, scale_ref, y_ref, nbuf_ref, ext_ref, *, tt, pos0):
    t = pl.program_id(1)

    @pl.when(t == 0)
    def _():
        ext_ref[0:POOL_HALO, :] = buf_ref[0]

    u = u_ref[0]
    ext_ref[POOL_HALO:POOL_HALO + tt, :] = u
    pos = pos0 + t * tt + lax.broadcasted_iota(jnp.int32, (tt, 1), 0)
    for gi, w in enumerate(POOL_WINDOWS):
        cols = slice(gi * G_A, (gi + 1) * G_A)
        win = u[:, cols]
        for j in range(1, w):
            win = win + ext_ref[POOL_HALO - j:POOL_HALO - j + tt, cols]
        cnt = jnp.minimum(pos + 1, w).astype(F32)
        pooled = win / cnt - u[:, cols]
        y = _dot(pooled.astype(BF16), w_ref[gi]) * scale_ref[:, cols]
        y_ref[0, :, cols] = y.astype(y_ref.dtype)
    tail = ext_ref[tt:tt + POOL_HALO, :]
    nbuf_ref[0] = tail
    ext_ref[0:POOL_HALO, :] = tail


def _pool_mix(u, buf16, w_pool, scale, *, pos0, tt):
    B, T, _ = u.shape
    tt = min(tt, T)
    assert T % tt == 0
    return pl.pallas_call(
        functools.partial(_pool_kernel, tt=tt, pos0=pos0),
        grid=(B, T // tt),
        in_specs=[pl.BlockSpec((1, tt, D_A), lambda b, t: (b, t, 0)),
                  pl.BlockSpec((1, POOL_HALO, D_A), lambda b, t: (b, 0, 0)),
                  pl.BlockSpec((len(POOL_WINDOWS), G_A, G_A), lambda b, t: (0, 0, 0)),
                  pl.BlockSpec((1, D_A), lambda b, t: (0, 0))],
        out_specs=[pl.BlockSpec((1, tt, D_A), lambda b, t: (b, t, 0)),
                   pl.BlockSpec((1, POOL_HALO, D_A), lambda b, t: (b, 0, 0))],
        out_shape=[jax.ShapeDtypeStruct((B, T, D_A), BF16),
                   jax.ShapeDtypeStruct((B, POOL_HALO, D_A), F32)],
        scratch_shapes=[pltpu.VMEM((POOL_HALO + tt, D_A), F32)],
        compiler_params=_cparams(("parallel", "arbitrary")),
        name="pool_mix",
    )(u, buf16, w_pool, scale)


def _hgrn_kernel(q_ref, f_ref, i_ref, g_ref, lbraw_ref, s0_ref, ng_ref, y_ref, sfin_ref,
                 st_ref, b_ref, kk_ref, v_ref, qd_ref, o_ref, *, layer, tt, t_valid):
    t = pl.program_id(1)
    nt = pl.num_programs(1)
    C = HGRN_CHUNK

    @pl.when(t == 0)
    def _():
        for h in range(H_B):
            st_ref[h] = s0_ref[0, h].T

    raw = lbraw_ref[...]
    e = jnp.exp(raw - jnp.max(raw, axis=0, keepdims=True))
    sm = e / jnp.sum(e, axis=0, keepdims=True)
    lb = jnp.zeros((1, D_B), F32)
    for j in range(1, layer + 1):
        lb = lb + sm[j:j + 1]
    lb = (lb + sm[0:1]) - sm[0:1]

    z = f_ref[0]
    la = jnp.log(lb)
    c = jnp.log1p(-lb) + _log_sigmoid(z)
    amax = jnp.maximum(la, c)
    delta = la - c
    lf = jnp.where(jnp.isnan(delta), la + c, amax + jnp.log1p(jnp.exp(-jnp.abs(delta))))
    kk = (1.0 - lb) * _sigmoid(-z)
    if t_valid is not None:
        row = t * tt + lax.broadcasted_iota(jnp.int32, (tt, 1), 0)
        lf = jnp.where(row < t_valid, lf, 0.0)
        kk = jnp.where(row < t_valid, kk, 0.0)
    r = lax.broadcasted_iota(jnp.int32, (tt, tt), 0)
    s = lax.broadcasted_iota(jnp.int32, (tt, tt), 1)
    tri = jnp.where((s <= r) & (s >= jnp.bitwise_and(r, -C)), 1.0, 0.0).astype(BF16)
    b_all = _dot01(tri, lf)
    qs_all = q_ref[0] * DK_B ** -0.5
    v_all = i_ref[0]
    heads = [slice(h * DK_B, (h + 1) * DK_B) for h in range(H_B)]
    halo = jnp.zeros((C, DK_B), F32)
    for h, hl in enumerate(heads):
        for ref, val in ((b_ref, b_all), (kk_ref, kk), (v_ref, v_all)):
            ref[h, 0:C, :] = halo
            ref[h, C:C + tt, :] = val[:, hl]

    tmod = jnp.bitwise_and(lax.broadcasted_iota(jnp.int32, (tt, 1), 0), C - 1)
    for h, hl in enumerate(heads):
        bh, qh = b_all[:, hl], qs_all[:, hl]
        acc = jnp.zeros((tt, DK_B), F32)
        for d in range(C):
            lag = slice(C - d, C - d + tt)
            w = qh * jnp.exp(jnp.minimum(bh - b_ref[h, lag, :], 0.0)) * kk_ref[h, lag, :]
            rs = jnp.sum(w, axis=-1, keepdims=True)
            acc = acc + jnp.where(tmod >= d, rs, 0.0) * v_ref[h, lag, :]
        o_ref[:, hl] = acc
        qd_ref[:, hl] = qh * jnp.exp(bh)

    def chunk(ci, carry):
        rows = pl.ds(pl.multiple_of(ci * C, C), C)
        rows_h = pl.ds(pl.multiple_of(ci * C + C, C), C)
        for h, hl in enumerate(heads):
            b = b_ref[h, rows_h, :]
            v = v_ref[h, rows_h, :]
            st = st_ref[h]
            o_ref[rows, hl] += _dot_nt(qd_ref[rows, hl].astype(BF16), st.astype(BF16))
            b_last = b[C - 1:C]
            k_dec = kk_ref[h, rows_h, :] * jnp.exp(b_last - b)
            st_ref[h] = st * jnp.exp(b_last) + _dot_tn(v.astype(BF16), k_dec.astype(BF16))
        return carry

    n_chunks = tt // C
    lax.fori_loop(0, n_chunks, chunk, 0, unroll=math.gcd(n_chunks, 4))

    for h, hl in enumerate(heads):
        o = o_ref[:, hl]
        o = o * lax.rsqrt(jnp.mean(jnp.square(o), axis=-1, keepdims=True) + RMS_EPS) * ng_ref[...]
        o = o * _sigmoid(g_ref[0, :, hl])
        y_ref[0, :, hl] = o.astype(y_ref.dtype)

    @pl.when(t == nt - 1)
    def _():
        for h in range(H_B):
            sfin_ref[0, h] = st_ref[h].T


def _hgrn2(zb, lb_raw, s0, norm_g, *, layer, tt, t_valid=None):
    B, T, _ = zb.shape
    tt = min(tt, T)
    assert T % tt == 0 and tt % HGRN_CHUNK == 0

    def col(j):
        return pl.BlockSpec((1, tt, D_B), lambda b, t: (b, t, j))

    return pl.pallas_call(
        functools.partial(_hgrn_kernel, layer=layer, tt=tt, t_valid=t_valid),
        grid=(B, T // tt),
        in_specs=[col(0), col(1), col(2), col(3),
                  pl.BlockSpec((DEPTH, D_B), lambda b, t: (0, 0)),
                  pl.BlockSpec((1, H_B, DK_B, DK_B), lambda b, t: (b, 0, 0, 0)),
                  pl.BlockSpec((1, DK_B), lambda b, t: (0, 0))],
        out_specs=[pl.BlockSpec((1, tt, D_B), lambda b, t: (b, t, 0)),
                   pl.BlockSpec((1, H_B, DK_B, DK_B), lambda b, t: (b, 0, 0, 0))],
        out_shape=[jax.ShapeDtypeStruct((B, T, D_B), BF16),
                   jax.ShapeDtypeStruct((B, H_B, DK_B, DK_B), F32)],
        scratch_shapes=[pltpu.VMEM((H_B, DK_B, DK_B), F32)]
        + [pltpu.VMEM((H_B, HGRN_CHUNK + tt, DK_B), F32)] * 3
        + [pltpu.VMEM((tt, D_B), F32)] * 2,
        compiler_params=_cparams(("parallel", "arbitrary")),
        name="hgrn2",
    )(zb, zb, zb, zb, lb_raw, s0, norm_g)


def _fox_gate_kernel(f_ref, fb_ref, logf_ref, c_ref, *, t_valid):
    S = f_ref.shape[1]
    logf = _log_sigmoid(f_ref[0] + fb_ref[...])
    logf_ref[0] = logf
    if t_valid is not None:
        row = lax.broadcasted_iota(jnp.int32, (S, 1), 0)
        logf = jnp.where(row < t_valid, logf, 0.0)
    r = lax.broadcasted_iota(jnp.int32, (S, S), 0)
    s = lax.broadcasted_iota(jnp.int32, (S, S), 1)
    tri = jnp.where(s <= r, 1.0, 0.0).astype(BF16)
    c_ref[0] = _dot01(tri, logf)


def _fox_gate(f_pad, fb_pad, *, t_valid=None):
    B, S, L = f_pad.shape
    spec = pl.BlockSpec((1, S, L), lambda b: (b, 0, 0))
    return pl.pallas_call(
        functools.partial(_fox_gate_kernel, t_valid=t_valid),
        grid=(B,),
        in_specs=[spec, pl.BlockSpec((1, L), lambda b: (0, 0))],
        out_specs=[spec, spec],
        out_shape=[jax.ShapeDtypeStruct((B, S, L), F32)] * 2,
        compiler_params=_cparams(("parallel",)),
        name="fox_gate",
    )(f_pad, fb_pad)


def _fox_prompt_kernel(q_ref, k_ref, v_ref, ccol_ref, crow_ref, o_ref, *, tq):
    h = pl.program_id(1)
    qi = pl.program_id(2)
    q = q_ref[0]
    lane = lax.broadcasted_iota(jnp.int32, (tq, LANES), 1)
    cq = jnp.sum(jnp.where(lane == h, ccol_ref[0], 0.0), axis=-1, keepdims=True)
    scale = DH_C ** -0.5

    def block(j, carry, masked):
        m, l, acc = carry
        ks = pl.ds(pl.multiple_of(j * tq, tq), tq)
        kb = k_ref[0, ks, :].astype(BF16)
        vb = v_ref[0, ks, :].astype(BF16)
        s = _dot_nt(q, kb) * scale
        s = s + (cq - crow_ref[0, 0, :, ks])
        if masked:
            rr = lax.broadcasted_iota(jnp.int32, (tq, tq), 0)
            cc = lax.broadcasted_iota(jnp.int32, (tq, tq), 1)
            s = jnp.where(cc <= rr, s, NEG_INF)
        m_new = jnp.maximum(m, jnp.max(s, axis=-1, keepdims=True))
        alpha = jnp.exp(m - m_new)
        p = jnp.exp(s - m_new)
        l = alpha * l + jnp.sum(p, axis=-1, keepdims=True)
        acc = alpha * acc + _dot(p.astype(BF16), vb)
        return m_new, l, acc

    init = (jnp.full((tq, 1), NEG_INF, F32), jnp.zeros((tq, 1), F32), jnp.zeros((tq, DH_C), F32))
    carry = block(qi, init, True)
    m, l, acc = lax.fori_loop(0, qi, lambda j, cr: block(j, cr, False), carry)
    o_ref[0] = (acc / l).astype(o_ref.dtype)


def _fox_prompt(q, k, v, c_col, c_row, *, tq=512):
    B, S, _ = q.shape
    tq = min(tq, S)
    assert S % tq == 0
    kv_spec = pl.BlockSpec((1, S, DH_C), lambda b, h, i: (b, 0, h))
    return pl.pallas_call(
        functools.partial(_fox_prompt_kernel, tq=tq),
        grid=(B, H_C, S // tq),
        in_specs=[pl.BlockSpec((1, tq, DH_C), lambda b, h, i: (b, i, h)), kv_spec, kv_spec,
                  pl.BlockSpec((1, tq, LANES), lambda b, h, i: (b, i, 0)),
                  pl.BlockSpec((1, 1, 1, S), lambda b, h, i: (b, h, 0, 0))],
        out_specs=pl.BlockSpec((1, tq, DH_C), lambda b, h, i: (b, i, h)),
        out_shape=jax.ShapeDtypeStruct((B, S, D_C), BF16),
        compiler_params=_cparams(("parallel", "parallel", "arbitrary")),
        name="fox_prompt",
    )(q, k, v, c_col, c_row)


def _fox_suffix_kernel(pt_ref, lf_ref, usel_ref, tsel_ref, esel_ref, o_ref, g_ref, *, n_pages):
    b = pl.program_id(0)
    for p in range(n_pages):
        g_ref[p:p + 1, :] = lf_ref[pl.ds(pt_ref[b, p], 1), :]
    g = g_ref[...]
    within = _dot01_rhs(g, usel_ref[...])
    tot = _dot01_rhs(g, tsel_ref[...])
    r = lax.broadcasted_iota(jnp.int32, (n_pages, n_pages), 0)
    s = lax.broadcasted_iota(jnp.int32, (n_pages, n_pages), 1)
    later = jnp.where(s > r, 1.0, 0.0).astype(BF16)
    carry = _dot01(later, tot)
    suf = within + _dot01_rhs(carry, esel_ref[...])
    for h in range(H_C):
        o_ref[0, h] = suf[:, h * PAGE_SIZE:(h + 1) * PAGE_SIZE]


def _dot01_rhs(x, sel):
    hi = x.astype(BF16)
    r1 = x - hi.astype(F32)
    mid = r1.astype(BF16)
    lo = (r1 - mid.astype(F32)).astype(BF16)
    return _dot(hi, sel) + _dot(mid, sel) + _dot(lo, sel)


def _fox_suffix(page_table, cache_logf_l):
    B, n_pages = page_table.shape
    n_pool, W = cache_logf_l.shape
    key_in = jnp.arange(W) // H_C
    head_in = jnp.arange(W) % H_C
    head_out = jnp.arange(W) // PAGE_SIZE
    key_out = jnp.arange(W) % PAGE_SIZE
    usel = ((head_in[:, None] == head_out[None, :]) & (key_in[:, None] > key_out[None, :])).astype(BF16)
    tsel = (head_in[:, None] == jnp.arange(LANES)[None, :]).astype(BF16)
    esel = (jnp.arange(LANES)[:, None] == head_out[None, :]).astype(BF16)
    full = lambda shp: pl.BlockSpec(shp, lambda b, pt: tuple(0 for _ in shp))
    grid_spec = pltpu.PrefetchScalarGridSpec(
        num_scalar_prefetch=1,
        grid=(B,),
        in_specs=[full((n_pool, W)), full((W, W)), full((W, LANES)), full((LANES, W))],
        out_specs=pl.BlockSpec((1, H_C, n_pages, PAGE_SIZE), lambda b, pt: (b, 0, 0, 0)),
        scratch_shapes=[pltpu.VMEM((n_pages, W), F32)],
    )
    return pl.pallas_call(
        functools.partial(_fox_suffix_kernel, n_pages=n_pages),
        grid_spec=grid_spec,
        out_shape=jax.ShapeDtypeStruct((B, H_C, n_pages, PAGE_SIZE), F32),
        compiler_params=_cparams(("arbitrary",)),
        name="fox_suffix",
    )(page_table, cache_logf_l, usel, tsel, esel)


def _fox_sample_kernel(pt_ref, q_ref, kn_ref, vn_ref, cq_ref, ckn_ref, suf_ref, *rest, t_new, ppg):
    k_refs = rest[:ppg]
    v_refs = rest[ppg:2 * ppg]
    o_ref, m_ref, l_ref, acc_ref = rest[2 * ppg:]
    g = pl.program_id(1)
    ng = pl.num_programs(1)
    scale = DH_C ** -0.5
    R = H_C * t_new

    def q_head(h):
        return q_ref[0, :, h * DH_C:(h + 1) * DH_C]

    def update(s, v_of_head):
        m = m_ref[...]
        m_new = jnp.maximum(m, jnp.max(s, axis=-1, keepdims=True))
        alpha = jnp.exp(m - m_new)
        p = jnp.exp(s - m_new)
        l_ref[...] = alpha * l_ref[...] + jnp.sum(p, axis=-1, keepdims=True)
        m_ref[...] = m_new
        for h in range(H_C):
            rows = slice(h * t_new, (h + 1) * t_new)
            acc_ref[rows, :] = alpha[rows] * acc_ref[rows, :] + _dot(p[rows].astype(BF16), v_of_head(h))

    @pl.when(g == 0)
    def _():
        m_ref[...] = jnp.full_like(m_ref, NEG_INF)
        l_ref[...] = jnp.zeros_like(l_ref)
        acc_ref[...] = jnp.zeros_like(acc_ref)
        s = jnp.concatenate(
            [_dot_nt(q_head(h), kn_ref[0, :, h * DH_C:(h + 1) * DH_C]) for h in range(H_C)], axis=0) * scale
        s = s + (cq_ref[0] - ckn_ref[0])
        qpos = lax.broadcasted_iota(jnp.int32, s.shape, 0) % t_new
        kpos = lax.broadcasted_iota(jnp.int32, s.shape, 1)
        s = jnp.where(kpos <= qpos, s, NEG_INF)
        update(s, lambda h: vn_ref[0, :, h * DH_C:(h + 1) * DH_C])

    def head_rows(ref, h):
        return ref[0, 0, pl.ds(h, PAGE_SIZE, stride=H_C), :].astype(BF16)

    for j in range(ppg):
        kj, vj = k_refs[j], v_refs[j]
        s = jnp.concatenate([_dot_nt(q_head(h), head_rows(kj, h)) for h in range(H_C)], axis=0) * scale
        bias = jnp.concatenate(
            [jnp.broadcast_to(suf_ref[0, h, pl.ds(g * ppg + j, 1), :], (t_new, PAGE_SIZE)) for h in range(H_C)], axis=0)
        s = s + (cq_ref[0] + bias)
        update(s, lambda h: head_rows(vj, h))

    @pl.when(g == ng - 1)
    def _():
        o = acc_ref[...] / l_ref[...]
        for h in range(H_C):
            o_ref[0, :, h * DH_C:(h + 1) * DH_C] = o[h * t_new:(h + 1) * t_new].astype(o_ref.dtype)


def _fox_sample(q, k_new, v_new, cq_col, ck_new, suffix, cache_k, cache_v, page_table, *, layer, t_new, ppg=4):
    B, n_pages = page_table.shape
    T = q.shape[1]
    T16 = k_new.shape[1]
    R = H_C * t_new
    assert n_pages % ppg == 0

    def page_spec(j):
        return pl.BlockSpec((1, 1, PAGE_SIZE * H_C, DH_C),
                            lambda b, g, pt: (layer, pt[b, g * ppg + j], 0, 0))

    bspec = lambda shp: pl.BlockSpec((1,) + shp, lambda b, g, pt: (b,) + tuple(0 for _ in shp))
    grid_spec = pltpu.PrefetchScalarGridSpec(
        num_scalar_prefetch=1,
        grid=(B, n_pages // ppg),
        in_specs=[bspec((T, D_C)), bspec((T16, D_C)), bspec((T16, D_C)), bspec((R, 1)), bspec((R, T16)),
                  bspec((H_C, n_pages, PAGE_SIZE))]
        + [page_spec(j) for j in range(ppg)] + [page_spec(j) for j in range(ppg)],
        out_specs=bspec((T, D_C)),
        scratch_shapes=[pltpu.VMEM((R, 1), F32), pltpu.VMEM((R, 1), F32), pltpu.VMEM((R, DH_C), F32)],
    )
    return pl.pallas_call(
        functools.partial(_fox_sample_kernel, t_new=t_new, ppg=ppg),
        grid_spec=grid_spec,
        out_shape=jax.ShapeDtypeStruct((B, T, D_C), BF16),
        compiler_params=_cparams(("parallel", "arbitrary")),
        name="fox_sample",
    )(page_table, q, k_new, v_new, cq_col, ck_new, suffix, *([cache_k] * ppg), *([cache_v] * ppg))


def _merge_kernel(ya_ref, yb_ref, yc_ref, wa_ref, wb_ref, wc_ref, ga_ref, gb_ref, gc_ref, o_ref):
    m = ga_ref[...].astype(F32) * _dot(ya_ref[...], wa_ref[...])
    m = m + gb_ref[...].astype(F32) * _dot(yb_ref[...], wb_ref[...])
    m = m + gc_ref[...].astype(F32) * _dot(yc_ref[...], wc_ref[...])
    o_ref[...] = m.astype(o_ref.dtype)


def _merge(y_a, y_b, y_c, w_a, w_b, w_c, gates, *, tm=1024, tn=512):
    M = y_a.shape[0]
    tm = min(tm, M)
    nj = D_MODEL // tn
    act = lambda K: pl.BlockSpec((tm, K), lambda i, j: (i, 0))
    wsp = lambda K: pl.BlockSpec((K, tn), lambda i, j: (0, j))
    gsp = lambda o: pl.BlockSpec((tm, tn), lambda i, j: (i, j + o * nj))
    return pl.pallas_call(
        _merge_kernel,
        grid=(M // tm, nj),
        in_specs=[act(D_A), act(D_B), act(D_C), wsp(D_A), wsp(D_B), wsp(D_C), gsp(0), gsp(1), gsp(2)],
        out_specs=pl.BlockSpec((tm, tn), lambda i, j: (i, j)),
        out_shape=jax.ShapeDtypeStruct((M, D_MODEL), BF16),
        compiler_params=_cparams(("parallel", "parallel")),
        name="merge",
    )(y_a, y_b, y_c, w_a, w_b, w_c, gates, gates, gates)


def _ln_kernel(*refs, n_delta):
    x_ref = refs[0]
    d_refs = refs[1:1 + n_delta]
    g_ref, b_ref, o_ref, ob_ref = refs[1 + n_delta:]
    y = DN_ALPHA * x_ref[...]
    for d in d_refs:
        y = y + d[...].astype(F32)
    mu = jnp.mean(y, axis=-1, keepdims=True)
    yc = y - mu
    var = jnp.mean(jnp.square(yc), axis=-1, keepdims=True)
    o = yc * lax.rsqrt(var + LN_EPS) * g_ref[...] + b_ref[...]
    o_ref[...] = o
    ob_ref[...] = o.astype(BF16)


def _res_ln(x, deltas, g, b, *, tm=128):
    M = x.shape[0]
    tm = min(tm, M)
    row = pl.BlockSpec((tm, D_MODEL), lambda i: (i, 0))
    vec = pl.BlockSpec((1, D_MODEL), lambda i: (0, 0))
    return pl.pallas_call(
        functools.partial(_ln_kernel, n_delta=len(deltas)),
        grid=(M // tm,),
        in_specs=[row] * (1 + len(deltas)) + [vec, vec],
        out_specs=[row, row],
        out_shape=[jax.ShapeDtypeStruct((M, D_MODEL), F32), jax.ShapeDtypeStruct((M, D_MODEL), BF16)],
        compiler_params=_cparams(("parallel",)),
        name="res_ln",
    )(x, *deltas, g, b)


OFF_A = 0
OFF_B = D_A
OFF_QC = OFF_B + 4 * D_B
OFF_KC = OFF_QC + D_C
OFF_VC = OFF_KC + D_C
OFF_FC = OFF_VC + D_C
N_GATES = 3 * D_MODEL


def _in_proj(xpb, xsb, w_in, layer):
    seg = lambda **kw: _proj(xpb, xsb, w_in, layer, **kw)
    parts = {
        "u_a": seg(n_off=OFF_A, n_out=D_A),
        "zb": seg(n_off=OFF_B, n_out=4 * D_B),
        "q_c": seg(n_off=OFF_QC, n_out=D_C, out_dtype=BF16),
        "k_c": seg(n_off=OFF_KC, n_out=D_C),
        "v_c": seg(n_off=OFF_VC, n_out=D_C),
        "f_c": seg(n_off=OFF_FC, n_out=LANES, tn=LANES),
        "gates": seg(n_off=OFF_FC, n_out=N_GATES, shift=H_C, out_dtype=BF16, act="sigmoid"),
    }
    return ({k: v[0] for k, v in parts.items()}, {k: v[1] for k, v in parts.items()})


def _mixers(z, lw, *, layer, B, T, pool_buf16, pos0, hgrn_s0, lb_raw, sample=None):
    M = B * T
    tm = min(PROJ_TM, M)
    u_a, zb, q_c, k_c, v_c, f_c, gates = (z[k] for k in ("u_a", "zb", "q_c", "k_c", "v_c", "f_c", "gates"))

    y_a, nbuf = _pool_mix(u_a.reshape(B, T, D_A), pool_buf16, lw["w_pool"], lw["pool_scale"], pos0=pos0, tt=256)
    new_buf = nbuf[:, 1:]

    zb3 = zb.reshape(B, T, 4 * D_B)
    if T % HGRN_CHUNK:
        Tp = -(-T // HGRN_CHUNK) * HGRN_CHUNK
        zb3 = jnp.pad(zb3, ((0, 0), (0, Tp - T), (0, 0)))
        y_b, new_s = _hgrn2(zb3, lb_raw, hgrn_s0, lw["hgrn_norm_g"], layer=layer, tt=128, t_valid=T)
        y_b = y_b[:, :T]
    else:
        y_b, new_s = _hgrn2(zb3, lb_raw, hgrn_s0, lw["hgrn_norm_g"], layer=layer, tt=128)

    q3, k3, v3 = (a.reshape(B, T, D_C) for a in (q_c, k_c, v_c))
    if sample is None:
        logf, c = _fox_gate(f_c.reshape(B, T, LANES), lw["fox_fb"])
        c_row = jnp.transpose(c[:, :, :H_C], (0, 2, 1)).reshape(B, H_C, 1, T)
        y_c = _fox_prompt(q3, k3, v3, c, c_row)
    else:
        T16 = PAGE_SIZE
        f3 = jnp.pad(f_c.reshape(B, T, LANES), ((0, 0), (0, T16 - T), (0, 0)))
        logf, c = _fox_gate(f3, lw["fox_fb"], t_valid=T)
        logf = logf[:, :T]
        c_hm = jnp.transpose(c[:, :, :H_C], (0, 2, 1))
        cq_col = c_hm[:, :, :T].reshape(B, H_C * T, 1)
        ck_new = jnp.repeat(c_hm, T, axis=1)
        pad_kv = lambda a: jnp.pad(a.astype(BF16), ((0, 0), (0, T16 - T), (0, 0)))
        suffix = _fox_suffix(sample["page_table"], sample["cache_logf"][layer])
        y_c = _fox_sample(q3, pad_kv(k3), pad_kv(v3), cq_col, ck_new, suffix, sample["cache_k"], sample["cache_v"],
                          sample["page_table"], layer=layer, t_new=T)
    logf = logf[:, :, :H_C]

    merged = _merge(y_a.reshape(M, D_A), y_b.reshape(M, D_B), y_c.reshape(M, D_C),
                    lw["w_up_a"], lw["w_up_b"], lw["w_up_c"], gates, tm=tm)
    return merged, new_buf, new_s, logf


def kernel(x_prompt, x_sample, cache_k, cache_v, cache_logf, state_hgrn, state_pool, page_table, p_prompt, p_sample,
           w_in, fox_fb, w_pool, pool_scale, hgrn_lb, hgrn_norm_g, w_up_a, w_up_b, w_up_c, w_out, ln1_g, ln1_b,
           w_ff_up, w_ff_down, w_ple, w_ple_gate, ln2_g, ln2_b):
    Bp, Tp, _ = x_prompt.shape
    Bs, Ts, _ = x_sample.shape
    n_pool = cache_k.shape[1]
    past_len = page_table.shape[1] * PAGE_SIZE
    cache_logf2 = cache_logf.reshape(DEPTH, n_pool, PAGE_SIZE * H_C)
    page_rows = (DEPTH, n_pool, PAGE_SIZE * H_C, DH_C)
    sample_ctx = {"page_table": page_table, "cache_k": cache_k.reshape(page_rows),
                  "cache_v": cache_v.reshape(page_rows), "cache_logf": cache_logf2}

    xp = x_prompt.reshape(Bp * Tp, D_MODEL)
    xs = x_sample.reshape(Bs * Ts, D_MODEL)
    xpb, xsb = xp.astype(BF16), xs.astype(BF16)
    zero_buf = jnp.zeros((Bp, POOL_HALO, D_A), F32)
    zero_state = jnp.zeros((Bp, H_B, DK_B, DK_B), F32)
    lb_raw = hgrn_lb.astype(F32)

    outs = [[] for _ in range(10)]
    for i in range(DEPTH):
        lw = {
            "fox_fb": jnp.pad(fox_fb[i].astype(F32), (0, LANES - H_C)).reshape(1, LANES),
            "w_pool": w_pool[i].astype(BF16),
            "pool_scale": pool_scale[i].astype(F32).reshape(1, D_A),
            "hgrn_norm_g": hgrn_norm_g[i].astype(F32).reshape(1, DK_B),
            "w_up_a": w_up_a[i].astype(BF16), "w_up_b": w_up_b[i].astype(BF16), "w_up_c": w_up_c[i].astype(BF16),
        }
        vec = lambda a: a[i].astype(F32).reshape(1, D_MODEL)
        pp = p_prompt[i].reshape(Bp * Tp, -1).astype(BF16)
        ps = p_sample[i].reshape(Bs * Ts, -1).astype(BF16)

        zp, zs = _in_proj(xpb, xsb, w_in, i)
        merged_p, buf_p, s_p, f_p = _mixers(
            zp, lw, layer=i, B=Bp, T=Tp, pool_buf16=zero_buf, pos0=0, hgrn_s0=zero_state, lb_raw=lb_raw)
        buf16 = jnp.pad(state_pool[i], ((0, 0), (POOL_HALO - POOL_BUF, 0), (0, 0)))
        merged_s, buf_s, s_s, f_s = _mixers(
            zs, lw, layer=i, B=Bs, T=Ts, pool_buf16=buf16, pos0=past_len, hgrn_s0=state_hgrn[i], lb_raw=lb_raw,
            sample=sample_ctx)

        mix_p, mix_s = _proj(merged_p, merged_s, w_out, i)
        hp, hpb = _res_ln(xp, [mix_p], vec(ln1_g), vec(ln1_b))
        hs, hsb = _res_ln(xs, [mix_s], vec(ln1_g), vec(ln1_b))
        up_p, up_s = _proj(hpb, hsb, w_ff_up, i, out_dtype=BF16, act="relu2")
        w_down = w_ff_down[i].astype(BF16)
        ff_p = _mm(up_p, w_down)
        ff_s = _mm(up_s, w_down)
        ple_p, ple_s = _proj(hpb, hsb, w_ple_gate, i, act="sigmoid", side=(pp, ps, w_ple))
        xp, xpb = _res_ln(hp, [ff_p, ple_p], vec(ln2_g), vec(ln2_b))
        xs, xsb = _res_ln(hs, [ff_s, ple_s], vec(ln2_g), vec(ln2_b))

        vals = (zp["k_c"].reshape(Bp, Tp, H_C, DH_C), zp["v_c"].reshape(Bp, Tp, H_C, DH_C), f_p,
                zs["k_c"].reshape(Bs, Ts, H_C, DH_C), zs["v_c"].reshape(Bs, Ts, H_C, DH_C), f_s,
                s_p, s_s, buf_p, buf_s)
        for o, val in zip(outs, vals):
            o.append(val)
    return (xp.reshape(Bp, Tp, D_MODEL), xs.reshape(Bs, Ts, D_MODEL)) + tuple(jnp.stack(o) for o in outs)
```

```python
import functools
import math

import jax
import jax.numpy as jnp
from jax import lax
from jax.experimental import pallas as pl
from jax.experimental.pallas import tpu as pltpu

F32 = jnp.float32
BF16 = jnp.bfloat16

D_MODEL = 4096
DEPTH = 2
PAGE_SIZE = 128
POOL_WINDOWS = (2, 4, 8, 16)
D_A = D_MODEL // 4
G_A = D_A // len(POOL_WINDOWS)
POOL_BUF = 15
D_B = D_MODEL // 4
DK_B = 128
H_B = D_B // DK_B
D_C = D_MODEL // 2
DH_C = 128
H_C = D_C // DH_C
D_FF = 4 * D_MODEL
DN_ALPHA = (2 * DEPTH) ** 0.25
LN_EPS = 1e-5
RMS_EPS = 1e-6

LANES = 128
SUBLANES = 8
VMEM_LIMIT = 58 * 1024 * 1024

HGRN_CHUNK = 16
POOL_HALO = 16

NEG_INF = float("-inf")


def _cparams(sem):
    return pltpu.CompilerParams(dimension_semantics=sem, vmem_limit_bytes=VMEM_LIMIT)


def _dot(a, b):
    return jnp.dot(a, b, preferred_element_type=F32)


def _dot_nt(a, b):
    return lax.dot_general(a, b, (((1,), (1,)), ((), ())), preferred_element_type=F32)


def _dot_tn(a, b):
    return lax.dot_general(a, b, (((0,), (0,)), ((), ())), preferred_element_type=F32)


def _dot01(sel, x):
    hi = x.astype(BF16)
    r1 = x - hi.astype(F32)
    mid = r1.astype(BF16)
    lo = (r1 - mid.astype(F32)).astype(BF16)
    return _dot(sel, hi) + _dot(sel, mid) + _dot(sel, lo)


def _log_sigmoid(z):
    return jnp.minimum(z, 0.0) - jnp.log1p(jnp.exp(-jnp.abs(z)))


def _sigmoid(z):
    return 1.0 / (1.0 + jnp.exp(-z))


def _mm_kernel(x_ref, w_ref, o_ref):
    @pl.when(pl.program_id(2) == 0)
    def _():
        o_ref[...] = jnp.zeros_like(o_ref)

    o_ref[...] += _dot(x_ref[...], w_ref[...])


def _mm(x, w, layer, *, tm=1024, tn=1024, tk=4096):
    M, K = x.shape
    N = w.shape[2]
    tm, tn, tk = min(tm, M), min(tn, N), min(tk, K)
    assert M % tm == 0 and N % tn == 0 and K % tk == 0
    return pl.pallas_call(
        _mm_kernel,
        grid=(M // tm, N // tn, K // tk),
        in_specs=[pl.BlockSpec((tm, tk), lambda i, j, k: (i, k)),
                  pl.BlockSpec((None, tk, tn), lambda i, j, k: (layer, k, j))],
        out_specs=pl.BlockSpec((tm, tn), lambda i, j, k: (i, j)),
        out_shape=jax.ShapeDtypeStruct((M, N), F32),
        compiler_params=_cparams(("parallel", "parallel", "arbitrary")),
        name="dense",
    )(x, w)


PROJ_TM = 512
PROJ_TN = 1024
CAST_ROWS = 256


def _proj_kernel(*refs, act, shift, has_side, transposed):
    refs = list(refs)
    xp_ref, xs_ref, wa_ref = refs[:3]
    del refs[:3]
    wb_ref = refs.pop(0) if shift else None
    if has_side:
        pp_ref, ps_ref, w2_ref = refs[:3]
        del refs[:3]
    op_ref, os_ref, wbf_ref = refs[:3]
    w2bf_ref = refs[3] if has_side else None
    i = pl.program_id(1)
    n_rows = wbf_ref.shape[0]
    mm = _dot_nt if transposed else _dot

    def finish(acc, p_ref, o_ref):
        if act == "sigmoid":
            acc = _sigmoid(acc)
        elif act == "relu2":
            acc = jnp.square(jnp.maximum(acc, 0.0))
        if has_side:
            acc = _dot(p_ref[...], w2bf_ref[...]) * acc
        o_ref[...] = acc.astype(o_ref.dtype)

    @pl.when(i == 0)
    def _():
        for r in range(0, n_rows - shift, CAST_ROWS):
            n = min(CAST_ROWS, n_rows - shift - r)
            wbf_ref[r:r + n, :] = wa_ref[shift + r:shift + r + n, :].astype(BF16)
        if shift:
            wbf_ref[n_rows - shift:n_rows, :] = wb_ref[...].astype(BF16)
        if has_side:
            w2bf_ref[...] = w2_ref[...].astype(BF16)
        finish(mm(xs_ref[...], wbf_ref[...]), ps_ref if has_side else None, os_ref)

    @pl.when(i > 0)
    def _():
        finish(mm(xp_ref[...], wbf_ref[...]), pp_ref if has_side else None, op_ref)


def _proj(xp, xs, w, layer, *, n_off=0, n_out=None, out_dtype=F32, act=None, shift=0, side=None, transposed=False,
          tm=PROJ_TM, tn=PROJ_TN):
    Mp, K = xp.shape
    Ms = xs.shape[0]
    n_all = w.shape[1] if transposed else w.shape[2]
    n_out = n_all - n_off if n_out is None else n_out
    tm = min(tm, Mp)
    assert Mp % tm == 0 and n_out % tn == 0 and n_off % tn == 0
    assert shift == 0 or (transposed and shift % 16 == 0 and tn % shift == 0)
    nmp, joff = Mp // tm, n_off // tn
    row = lambda i: jnp.maximum(i - 1, 0)
    if transposed:
        w_spec = pl.BlockSpec((None, tn, K), lambda j, i: (layer, j + joff, 0))
    else:
        w_spec = pl.BlockSpec((None, K, tn), lambda j, i: (layer, 0, j + joff))
    in_specs = [pl.BlockSpec((tm, K), lambda j, i: (row(i), 0)), pl.BlockSpec((Ms, K), lambda j, i: (0, 0)), w_spec]
    args = [xp, xs, w]
    if shift:
        per = tn // shift
        in_specs.append(pl.BlockSpec((None, shift, K), lambda j, i: (layer, (j + joff + 1) * per, 0)))
        args.append(w)
    scratch = [pltpu.VMEM((tn, K) if transposed else (K, tn), BF16)]
    if side is not None:
        pp, ps, w2 = side
        K2 = pp.shape[1]
        in_specs += [pl.BlockSpec((tm, K2), lambda j, i: (row(i), 0)),
                     pl.BlockSpec((Ms, K2), lambda j, i: (0, 0)),
                     pl.BlockSpec((None, K2, tn), lambda j, i: (layer, 0, j))]
        args += [pp, ps, w2]
        scratch.append(pltpu.VMEM((K2, tn), BF16))
    return pl.pallas_call(
        functools.partial(_proj_kernel, act=act, shift=shift, has_side=side is not None, transposed=transposed),
        grid=(n_out // tn, nmp + 1),
        in_specs=in_specs,
        out_specs=[pl.BlockSpec((tm, tn), lambda j, i: (row(i), j)),
                   pl.BlockSpec((Ms, tn), lambda j, i: (0, j))],
        out_shape=[jax.ShapeDtypeStruct((Mp, n_out), out_dtype), jax.ShapeDtypeStruct((Ms, n_out), out_dtype)],
        scratch_shapes=scratch,
        compiler_params=_cparams(("parallel", "arbitrary")),
        name="proj",
    )(*args)


def _pool_kernel(u_ref, buf_ref, w_ref, scale_ref, y_ref, nbuf_ref, ext_ref, *, tt, pos0):
    t = pl.program_id(1)

    @pl.when(t == 0)
    def _():
        ext_ref[0:POOL_HALO, :] = buf_ref[0]

    u = u_ref[0]
    ext_ref[POOL_HALO:POOL_HALO + tt, :] = u
    pos = pos0 + t * tt + lax.broadcasted_iota(jnp.int32, (tt, 1), 0)
    for gi, w in enumerate(POOL_WINDOWS):
        cols = slice(gi * G_A, (gi + 1) * G_A)
        win = u[:, cols]
        for j in range(1, w):
            win = win + ext_ref[POOL_HALO - j:POOL_HALO - j + tt, cols]
        cnt = jnp.minimum(pos + 1, w).astype(F32)
        pooled = win / cnt - u[:, cols]
        y = _dot(pooled.astype(BF16), w_ref[gi]) * scale_ref[:, cols]
        y_ref[0, :, cols] = y.astype(y_ref.dtype)
    tail = ext_ref[tt:tt + POOL_HALO, :]
    nbuf_ref[0] = tail
    ext_ref[0:POOL_HALO, :] = tail


def _pool_mix(u, buf16, w_pool, scale, *, pos0, tt):
    B, T, _ = u.shape
    tt = min(tt, T)
    assert T % tt == 0
    return pl.pallas_call(
        functools.partial(_pool_kernel, tt=tt, pos0=pos0),
        grid=(B, T // tt),
        in_specs=[pl.BlockSpec((1, tt, D_A), lambda b, t: (b, t, 0)),
                  pl.BlockSpec((1, POOL_HALO, D_A), lambda b, t: (b, 0, 0)),
                  pl.BlockSpec((len(POOL_WINDOWS), G_A, G_A), lambda b, t: (0, 0, 0)),
                  pl.BlockSpec((1, D_A), lambda b, t: (0, 0))],
        out_specs=[pl.BlockSpec((1, tt, D_A), lambda b, t: (b, t, 0)),
                   pl.BlockSpec((1, POOL_HALO, D_A), lambda b, t: (b, 0, 0))],
        out_shape=[jax.ShapeDtypeStruct((B, T, D_A), BF16),
                   jax.ShapeDtypeStruct((B, POOL_HALO, D_A), F32)],
        scratch_shapes=[pltpu.VMEM((POOL_HALO + tt, D_A), F32)],
        compiler_params=_cparams(("parallel", "arbitrary")),
        name="pool_mix",
    )(u, buf16, w_pool, scale)


def _hgrn_kernel(q_ref, f_ref, i_ref, g_ref, lbraw_ref, s0_ref, ng_ref, y_ref, sfin_ref,
                 st_ref, b_ref, kk_ref, v_ref, qd_ref, o_ref, *, layer, tt, t_valid):
    t = pl.program_id(1)
    nt = pl.num_programs(1)
    C = HGRN_CHUNK

    @pl.when(t == 0)
    def _():
        for h in range(H_B):
            st_ref[h] = s0_ref[0, h].T

    raw = lbraw_ref[...]
    e = jnp.exp(raw - jnp.max(raw, axis=0, keepdims=True))
    sm = e / jnp.sum(e, axis=0, keepdims=True)
    lb = jnp.zeros((1, D_B), F32)
    for j in range(1, layer + 1):
        lb = lb + sm[j:j + 1]
    lb = (lb + sm[0:1]) - sm[0:1]

    z = f_ref[0]
    la = jnp.log(lb)
    c = jnp.log1p(-lb) + _log_sigmoid(z)
    amax = jnp.maximum(la, c)
    delta = la - c
    lf = jnp.where(jnp.isnan(delta), la + c, amax + jnp.log1p(jnp.exp(-jnp.abs(delta))))
    kk = (1.0 - lb) * _sigmoid(-z)
    if t_valid is not None:
        row = t * tt + lax.broadcasted_iota(jnp.int32, (tt, 1), 0)
        lf = jnp.where(row < t_valid, lf, 0.0)
        kk = jnp.where(row < t_valid, kk, 0.0)
    r = lax.broadcasted_iota(jnp.int32, (tt, tt), 0)
    s = lax.broadcasted_iota(jnp.int32, (tt, tt), 1)
    tri = jnp.where((s <= r) & (s >= jnp.bitwise_and(r, -C)), 1.0, 0.0).astype(BF16)
    b_all = _dot01(tri, lf)
    qs_all = q_ref[0] * DK_B ** -0.5
    v_all = i_ref[0]
    heads = [slice(h * DK_B, (h + 1) * DK_B) for h in range(H_B)]
    halo = jnp.zeros((C, DK_B), F32)
    for h, hl in enumerate(heads):
        for ref, val in ((b_ref, b_all), (kk_ref, kk), (v_ref, v_all)):
            ref[h, 0:C, :] = halo
            ref[h, C:C + tt, :] = val[:, hl]

    tmod = jnp.bitwise_and(lax.broadcasted_iota(jnp.int32, (tt, 1), 0), C - 1)
    for h, hl in enumerate(heads):
        bh, qh = b_all[:, hl], qs_all[:, hl]
        acc = jnp.zeros((tt, DK_B), F32)
        for d in range(C):
            lag = slice(C - d, C - d + tt)
            w = qh * jnp.exp(jnp.minimum(bh - b_ref[h, lag, :], 0.0)) * kk_ref[h, lag, :]
            rs = jnp.sum(w, axis=-1, keepdims=True)
            acc = acc + jnp.where(tmod >= d, rs, 0.0) * v_ref[h, lag, :]
        o_ref[:, hl] = acc
        qd_ref[:, hl] = qh * jnp.exp(bh)

    def chunk(ci, carry):
        rows = pl.ds(pl.multiple_of(ci * C, C), C)
        rows_h = pl.ds(pl.multiple_of(ci * C + C, C), C)
        for h, hl in enumerate(heads):
            b = b_ref[h, rows_h, :]
            v = v_ref[h, rows_h, :]
            st = st_ref[h]
            o_ref[rows, hl] += _dot_nt(qd_ref[rows, hl].astype(BF16), st.astype(BF16))
            b_last = b[C - 1:C]
            k_dec = kk_ref[h, rows_h, :] * jnp.exp(b_last - b)
            st_ref[h] = st * jnp.exp(b_last) + _dot_tn(v.astype(BF16), k_dec.astype(BF16))
        return carry

    n_chunks = tt // C
    lax.fori_loop(0, n_chunks, chunk, 0, unroll=math.gcd(n_chunks, 4))

    for h, hl in enumerate(heads):
        o = o_ref[:, hl]
        o = o * lax.rsqrt(jnp.mean(jnp.square(o), axis=-1, keepdims=True) + RMS_EPS) * ng_ref[...]
        o = o * _sigmoid(g_ref[0, :, hl])
        y_ref[0, :, hl] = o.astype(y_ref.dtype)

    @pl.when(t == nt - 1)
    def _():
        for h in range(H_B):
            sfin_ref[0, h] = st_ref[h].T


def _hgrn2(zb, lb_raw, s0, norm_g, *, layer, tt, t_valid=None):
    B, T, _ = zb.shape
    tt = min(tt, T)
    assert T % tt == 0 and tt % HGRN_CHUNK == 0

    def col(j):
        return pl.BlockSpec((1, tt, D_B), lambda b, t: (b, t, j))

    return pl.pallas_call(
        functools.partial(_hgrn_kernel, layer=layer, tt=tt, t_valid=t_valid),
        grid=(B, T // tt),
        in_specs=[col(0), col(1), col(2), col(3),
                  pl.BlockSpec((DEPTH, D_B), lambda b, t: (0, 0)),
                  pl.BlockSpec((1, H_B, DK_B, DK_B), lambda b, t: (b, 0, 0, 0)),
                  pl.BlockSpec((1, DK_B), lambda b, t: (0, 0))],
        out_specs=[pl.BlockSpec((1, tt, D_B), lambda b, t: (b, t, 0)),
                   pl.BlockSpec((1, H_B, DK_B, DK_B), lambda b, t: (b, 0, 0, 0))],
        out_shape=[jax.ShapeDtypeStruct((B, T, D_B), BF16),
                   jax.ShapeDtypeStruct((B, H_B, DK_B, DK_B), F32)],
        scratch_shapes=[pltpu.VMEM((H_B, DK_B, DK_B), F32)]
        + [pltpu.VMEM((H_B, HGRN_CHUNK + tt, DK_B), F32)] * 3
        + [pltpu.VMEM((tt, D_B), F32)] * 2,
        compiler_params=_cparams(("parallel", "arbitrary")),
        name="hgrn2",
    )(zb, zb, zb, zb, lb_raw, s0, norm_g)


def _fox_gate_kernel(f_ref, fb_ref, logf_ref, c_ref, *, t_valid):
    S = f_ref.shape[1]
    logf = _log_sigmoid(f_ref[0] + fb_ref[...])
    logf_ref[0] = logf
    if t_valid is not None:
        row = lax.broadcasted_iota(jnp.int32, (S, 1), 0)
        logf = jnp.where(row < t_valid, logf, 0.0)
    r = lax.broadcasted_iota(jnp.int32, (S, S), 0)
    s = lax.broadcasted_iota(jnp.int32, (S, S), 1)
    tri = jnp.where(s <= r, 1.0, 0.0).astype(BF16)
    c_ref[0] = _dot01(tri, logf)


def _fox_gate(f_pad, fb_pad, *, t_valid=None):
    B, S, L = f_pad.shape
    spec = pl.BlockSpec((1, S, L), lambda b: (b, 0, 0))
    return pl.pallas_call(
        functools.partial(_fox_gate_kernel, t_valid=t_valid),
        grid=(B,),
        in_specs=[spec, pl.BlockSpec((1, L), lambda b: (0, 0))],
        out_specs=[spec, spec],
        out_shape=[jax.ShapeDtypeStruct((B, S, L), F32)] * 2,
        compiler_params=_cparams(("parallel",)),
        name="fox_gate",
    )(f_pad, fb_pad)


def _fox_prompt_kernel(q_ref, k_ref, v_ref, ccol_ref, crow_ref, o_ref, *, tq):
    h = pl.program_id(1)
    qi = pl.program_id(2)
    q = q_ref[0]
    lane = lax.broadcasted_iota(jnp.int32, (tq, LANES), 1)
    cq = jnp.sum(jnp.where(lane == h, ccol_ref[0], 0.0), axis=-1, keepdims=True)
    scale = DH_C ** -0.5

    def block(j, carry, masked):
        m, l, acc = carry
        ks = pl.ds(pl.multiple_of(j * tq, tq), tq)
        kb = k_ref[0, ks, :].astype(BF16)
        vb = v_ref[0, ks, :].astype(BF16)
        s = _dot_nt(q, kb) * scale
        s = s + (cq - crow_ref[0, 0, :, ks])
        if masked:
            rr = lax.broadcasted_iota(jnp.int32, (tq, tq), 0)
            cc = lax.broadcasted_iota(jnp.int32, (tq, tq), 1)
            s = jnp.where(cc <= rr, s, NEG_INF)
        m_new = jnp.maximum(m, jnp.max(s, axis=-1, keepdims=True))
        alpha = jnp.exp(m - m_new)
        p = jnp.exp(s - m_new)
        l = alpha * l + jnp.sum(p, axis=-1, keepdims=True)
        acc = alpha * acc + _dot(p.astype(BF16), vb)
        return m_new, l, acc

    init = (jnp.full((tq, 1), NEG_INF, F32), jnp.zeros((tq, 1), F32), jnp.zeros((tq, DH_C), F32))
    carry = block(qi, init, True)
    m, l, acc = lax.fori_loop(0, qi, lambda j, cr: block(j, cr, False), carry)
    o_ref[0] = (acc / l).astype(o_ref.dtype)


def _fox_prompt(q, k, v, c_col, c_row, *, tq=512):
    B, S, _ = q.shape
    tq = min(tq, S)
    assert S % tq == 0
    kv_spec = pl.BlockSpec((1, S, DH_C), lambda b, h, i: (b, 0, h))
    return pl.pallas_call(
        functools.partial(_fox_prompt_kernel, tq=tq),
        grid=(B, H_C, S // tq),
        in_specs=[pl.BlockSpec((1, tq, DH_C), lambda b, h, i: (b, i, h)), kv_spec, kv_spec,
                  pl.BlockSpec((1, tq, LANES), lambda b, h, i: (b, i, 0)),
                  pl.BlockSpec((1, 1, 1, S), lambda b, h, i: (b, h, 0, 0))],
        out_specs=pl.BlockSpec((1, tq, DH_C), lambda b, h, i: (b, i, h)),
        out_shape=jax.ShapeDtypeStruct((B, S, D_C), BF16),
        compiler_params=_cparams(("parallel", "parallel", "arbitrary")),
        name="fox_prompt",
    )(q, k, v, c_col, c_row)


def _fox_suffix_kernel(pt_ref, lf_ref, usel_ref, tsel_ref, esel_ref, o_ref, g_ref, *, n_pages):
    b = pl.program_id(0)
    for p in range(n_pages):
        g_ref[p:p + 1, :] = lf_ref[pl.ds(pt_ref[b, p], 1), :]
    g = g_ref[...]
    within = _dot01_rhs(g, usel_ref[...])
    tot = _dot01_rhs(g, tsel_ref[...])
    r = lax.broadcasted_iota(jnp.int32, (n_pages, n_pages), 0)
    s = lax.broadcasted_iota(jnp.int32, (n_pages, n_pages), 1)
    later = jnp.where(s > r, 1.0, 0.0).astype(BF16)
    carry = _dot01(later, tot)
    suf = within + _dot01_rhs(carry, esel_ref[...])
    for h in range(H_C):
        o_ref[0, h] = suf[:, h * PAGE_SIZE:(h + 1) * PAGE_SIZE]


def _dot01_rhs(x, sel):
    hi = x.astype(BF16)
    r1 = x - hi.astype(F32)
    mid = r1.astype(BF16)
    lo = (r1 - mid.astype(F32)).astype(BF16)
    return _dot(hi, sel) + _dot(mid, sel) + _dot(lo, sel)


def _fox_suffix(page_table, cache_logf_l):
    B, n_pages = page_table.shape
    n_pool, W = cache_logf_l.shape
    key_in = jnp.arange(W) // H_C
    head_in = jnp.arange(W) % H_C
    head_out = jnp.arange(W) // PAGE_SIZE
    key_out = jnp.arange(W) % PAGE_SIZE
    usel = ((head_in[:, None] == head_out[None, :]) & (key_in[:, None] > key_out[None, :])).astype(BF16)
    tsel = (head_in[:, None] == jnp.arange(LANES)[None, :]).astype(BF16)
    esel = (jnp.arange(LANES)[:, None] == head_out[None, :]).astype(BF16)
    full = lambda shp: pl.BlockSpec(shp, lambda b, pt: tuple(0 for _ in shp))
    grid_spec = pltpu.PrefetchScalarGridSpec(
        num_scalar_prefetch=1,
        grid=(B,),
        in_specs=[full((n_pool, W)), full((W, W)), full((W, LANES)), full((LANES, W))],
        out_specs=pl.BlockSpec((1, H_C, n_pages, PAGE_SIZE), lambda b, pt: (b, 0, 0, 0)),
        scratch_shapes=[pltpu.VMEM((n_pages, W), F32)],
    )
    return pl.pallas_call(
        functools.partial(_fox_suffix_kernel, n_pages=n_pages),
        grid_spec=grid_spec,
        out_shape=jax.ShapeDtypeStruct((B, H_C, n_pages, PAGE_SIZE), F32),
        compiler_params=_cparams(("arbitrary",)),
        name="fox_suffix",
    )(page_table, cache_logf_l, usel, tsel, esel)


def _fox_sample_kernel(pt_ref, q_ref, kn_ref, vn_ref, cq_ref, ckn_ref, suf_ref, *rest, t_new, ppg):
    k_refs = rest[:ppg]
    v_refs = rest[ppg:2 * ppg]
    o_ref, m_ref, l_ref, acc_ref = rest[2 * ppg:]
    g = pl.program_id(1)
    ng = pl.num_programs(1)
    scale = DH_C ** -0.5
    R = H_C * t_new

    def q_head(h):
        return q_ref[0, :, h * DH_C:(h + 1) * DH_C]

    def update(s, v_of_head):
        m = m_ref[...]
        m_new = jnp.maximum(m, jnp.max(s, axis=-1, keepdims=True))
        alpha = jnp.exp(m - m_new)
        p = jnp.exp(s - m_new)
        l_ref[...] = alpha * l_ref[...] + jnp.sum(p, axis=-1, keepdims=True)
        m_ref[...] = m_new
        for h in range(H_C):
            rows = slice(h * t_new, (h + 1) * t_new)
            acc_ref[rows, :] = alpha[rows] * acc_ref[rows, :] + _dot(p[rows].astype(BF16), v_of_head(h))

    @pl.when(g == 0)
    def _():
        m_ref[...] = jnp.full_like(m_ref, NEG_INF)
        l_ref[...] = jnp.zeros_like(l_ref)
        acc_ref[...] = jnp.zeros_like(acc_ref)
        s = jnp.concatenate(
            [_dot_nt(q_head(h), kn_ref[0, :, h * DH_C:(h + 1) * DH_C]) for h in range(H_C)], axis=0) * scale
        s = s + (cq_ref[0] - ckn_ref[0])
        qpos = lax.broadcasted_iota(jnp.int32, s.shape, 0) % t_new
        kpos = lax.broadcasted_iota(jnp.int32, s.shape, 1)
        s = jnp.where(kpos <= qpos, s, NEG_INF)
        update(s, lambda h: vn_ref[0, :, h * DH_C:(h + 1) * DH_C])

    def head_rows(ref, h):
        return ref[0, 0, pl.ds(h, PAGE_SIZE, stride=H_C), :].astype(BF16)

    for j in range(ppg):
        kj, vj = k_refs[j], v_refs[j]
        s = jnp.concatenate([_dot_nt(q_head(h), head_rows(kj, h)) for h in range(H_C)], axis=0) * scale
        bias = jnp.concatenate(
            [jnp.broadcast_to(suf_ref[0, h, pl.ds(g * ppg + j, 1), :], (t_new, PAGE_SIZE)) for h in range(H_C)], axis=0)
        s = s + (cq_ref[0] + bias)
        update(s, lambda h: head_rows(vj, h))

    @pl.when(g == ng - 1)
    def _():
        o = acc_ref[...] / l_ref[...]
        for h in range(H_C):
            o_ref[0, :, h * DH_C:(h + 1) * DH_C] = o[h * t_new:(h + 1) * t_new].astype(o_ref.dtype)


def _fox_sample(q, k_new, v_new, cq_col, ck_new, suffix, cache_k, cache_v, page_table, *, layer, t_new, ppg=4):
    B, n_pages = page_table.shape
    T = q.shape[1]
    T16 = k_new.shape[1]
    R = H_C * t_new
    assert n_pages % ppg == 0

    def page_spec(j):
        return pl.BlockSpec((1, 1, PAGE_SIZE * H_C, DH_C),
                            lambda b, g, pt: (layer, pt[b, g * ppg + j], 0, 0))

    bspec = lambda shp: pl.BlockSpec((1,) + shp, lambda b, g, pt: (b,) + tuple(0 for _ in shp))
    grid_spec = pltpu.PrefetchScalarGridSpec(
        num_scalar_prefetch=1,
        grid=(B, n_pages // ppg),
        in_specs=[bspec((T, D_C)), bspec((T16, D_C)), bspec((T16, D_C)), bspec((R, 1)), bspec((R, T16)),
                  bspec((H_C, n_pages, PAGE_SIZE))]
        + [page_spec(j) for j in range(ppg)] + [page_spec(j) for j in range(ppg)],
        out_specs=bspec((T, D_C)),
        scratch_shapes=[pltpu.VMEM((R, 1), F32), pltpu.VMEM((R, 1), F32), pltpu.VMEM((R, DH_C), F32)],
    )
    return pl.pallas_call(
        functools.partial(_fox_sample_kernel, t_new=t_new, ppg=ppg),
        grid_spec=grid_spec,
        out_shape=jax.ShapeDtypeStruct((B, T, D_C), BF16),
        compiler_params=_cparams(("parallel", "arbitrary")),
        name="fox_sample",
    )(page_table, q, k_new, v_new, cq_col, ck_new, suffix, *([cache_k] * ppg), *([cache_v] * ppg))


def _merge_kernel(ya_ref, yb_ref, yc_ref, wa_ref, wb_ref, wc_ref, ga_ref, gb_ref, gc_ref, o_ref):
    m = ga_ref[...].astype(F32) * _dot(ya_ref[...], wa_ref[...])
    m = m + gb_ref[...].astype(F32) * _dot(yb_ref[...], wb_ref[...])
    m = m + gc_ref[...].astype(F32) * _dot(yc_ref[...], wc_ref[...])
    o_ref[...] = m.astype(o_ref.dtype)


def _merge(y_a, y_b, y_c, w_a, w_b, w_c, gates, *, tm=1024, tn=512):
    M = y_a.shape[0]
    tm = min(tm, M)
    nj = D_MODEL // tn
    act = lambda K: pl.BlockSpec((tm, K), lambda i, j: (i, 0))
    wsp = lambda K: pl.BlockSpec((K, tn), lambda i, j: (0, j))
    gsp = lambda o: pl.BlockSpec((tm, tn), lambda i, j: (i, j + o * nj))
    return pl.pallas_call(
        _merge_kernel,
        grid=(M // tm, nj),
        in_specs=[act(D_A), act(D_B), act(D_C), wsp(D_A), wsp(D_B), wsp(D_C), gsp(0), gsp(1), gsp(2)],
        out_specs=pl.BlockSpec((tm, tn), lambda i, j: (i, j)),
        out_shape=jax.ShapeDtypeStruct((M, D_MODEL), BF16),
        compiler_params=_cparams(("parallel", "parallel")),
        name="merge",
    )(y_a, y_b, y_c, w_a, w_b, w_c, gates, gates, gates)


def _ln_kernel(*refs, n_delta):
    x_ref = refs[0]
    d_refs = refs[1:1 + n_delta]
    g_ref, b_ref, o_ref, ob_ref = refs[1 + n_delta:]
    y = DN_ALPHA * x_ref[...]
    for d in d_refs:
        y = y + d[...].astype(F32)
    mu = jnp.mean(y, axis=-1, keepdims=True)
    yc = y - mu
    var = jnp.mean(jnp.square(yc), axis=-1, keepdims=True)
    o = yc * lax.rsqrt(var + LN_EPS) * g_ref[...] + b_ref[...]
    o_ref[...] = o
    ob_ref[...] = o.astype(BF16)


def _res_ln(x, deltas, g, b, *, tm=128):
    M = x.shape[0]
    tm = min(tm, M)
    row = pl.BlockSpec((tm, D_MODEL), lambda i: (i, 0))
    vec = pl.BlockSpec((1, D_MODEL), lambda i: (0, 0))
    return pl.pallas_call(
        functools.partial(_ln_kernel, n_delta=len(deltas)),
        grid=(M // tm,),
        in_specs=[row] * (1 + len(deltas)) + [vec, vec],
        out_specs=[row, row],
        out_shape=[jax.ShapeDtypeStruct((M, D_MODEL), F32), jax.ShapeDtypeStruct((M, D_MODEL), BF16)],
        compiler_params=_cparams(("parallel",)),
        name="res_ln",
    )(x, *deltas, g, b)


OFF_A = 0
OFF_B = D_A
OFF_QC = OFF_B + 4 * D_B
OFF_KC = OFF_QC + D_C
OFF_VC = OFF_KC + D_C
OFF_FC = OFF_VC + D_C
N_GATES = 3 * D_MODEL


def _in_proj(xpb, xsb, w_in_t, layer):
    seg = lambda **kw: _proj(xpb, xsb, w_in_t, layer, transposed=True, **kw)
    parts = {
        "u_a": seg(n_off=OFF_A, n_out=D_A),
        "zb": seg(n_off=OFF_B, n_out=4 * D_B),
        "q_c": seg(n_off=OFF_QC, n_out=D_C, out_dtype=BF16),
        "k_c": seg(n_off=OFF_KC, n_out=D_C),
        "v_c": seg(n_off=OFF_VC, n_out=D_C),
        "f_c": seg(n_off=OFF_FC, n_out=LANES, tn=LANES),
        "gates": seg(n_off=OFF_FC, n_out=N_GATES, shift=H_C, out_dtype=BF16, act="sigmoid"),
    }
    return ({k: v[0] for k, v in parts.items()}, {k: v[1] for k, v in parts.items()})


def _mixers(z, lw, *, layer, B, T, pool_buf16, pos0, hgrn_s0, lb_raw, sample=None):
    M = B * T
    tm = min(PROJ_TM, M)
    u_a, zb, q_c, k_c, v_c, f_c, gates = (z[k] for k in ("u_a", "zb", "q_c", "k_c", "v_c", "f_c", "gates"))

    y_a, nbuf = _pool_mix(u_a.reshape(B, T, D_A), pool_buf16, lw["w_pool"], lw["pool_scale"], pos0=pos0, tt=256)
    new_buf = nbuf[:, 1:]

    zb3 = zb.reshape(B, T, 4 * D_B)
    if T % HGRN_CHUNK:
        Tp = -(-T // HGRN_CHUNK) * HGRN_CHUNK
        zb3 = jnp.pad(zb3, ((0, 0), (0, Tp - T), (0, 0)))
        y_b, new_s = _hgrn2(zb3, lb_raw, hgrn_s0, lw["hgrn_norm_g"], layer=layer, tt=128, t_valid=T)
        y_b = y_b[:, :T]
    else:
        y_b, new_s = _hgrn2(zb3, lb_raw, hgrn_s0, lw["hgrn_norm_g"], layer=layer, tt=128)

    q3, k3, v3 = (a.reshape(B, T, D_C) for a in (q_c, k_c, v_c))
    if sample is None:
        logf, c = _fox_gate(f_c.reshape(B, T, LANES), lw["fox_fb"])
        c_row = jnp.transpose(c[:, :, :H_C], (0, 2, 1)).reshape(B, H_C, 1, T)
        y_c = _fox_prompt(q3, k3, v3, c, c_row)
    else:
        T16 = PAGE_SIZE
        f3 = jnp.pad(f_c.reshape(B, T, LANES), ((0, 0), (0, T16 - T), (0, 0)))
        logf, c = _fox_gate(f3, lw["fox_fb"], t_valid=T)
        logf = logf[:, :T]
        c_hm = jnp.transpose(c[:, :, :H_C], (0, 2, 1))
        cq_col = c_hm[:, :, :T].reshape(B, H_C * T, 1)
        ck_new = jnp.repeat(c_hm, T, axis=1)
        pad_kv = lambda a: jnp.pad(a.astype(BF16), ((0, 0), (0, T16 - T), (0, 0)))
        suffix = _fox_suffix(sample["page_table"], sample["cache_logf"][layer])
        y_c = _fox_sample(q3, pad_kv(k3), pad_kv(v3), cq_col, ck_new, suffix, sample["cache_k"], sample["cache_v"],
                          sample["page_table"], layer=layer, t_new=T)
    logf = logf[:, :, :H_C]

    merged = _merge(y_a.reshape(M, D_A), y_b.reshape(M, D_B), y_c.reshape(M, D_C),
                    lw["w_up_a"], lw["w_up_b"], lw["w_up_c"], gates, tm=tm)
    return merged, new_buf, new_s, logf


def kernel(x_prompt, x_sample, cache_k, cache_v, cache_logf, state_hgrn, state_pool, page_table, p_prompt, p_sample,
           w_in, fox_fb, w_pool, pool_scale, hgrn_lb, hgrn_norm_g, w_up_a, w_up_b, w_up_c, w_out, ln1_g, ln1_b,
           w_ff_up, w_ff_down, w_ple, w_ple_gate, ln2_g, ln2_b):
    Bp, Tp, _ = x_prompt.shape
    Bs, Ts, _ = x_sample.shape
    n_pool = cache_k.shape[1]
    past_len = page_table.shape[1] * PAGE_SIZE
    cache_logf2 = cache_logf.reshape(DEPTH, n_pool, PAGE_SIZE * H_C)
    page_rows = (DEPTH, n_pool, PAGE_SIZE * H_C, DH_C)
    sample_ctx = {"page_table": page_table, "cache_k": cache_k.reshape(page_rows),
                  "cache_v": cache_v.reshape(page_rows), "cache_logf": cache_logf2}

    xp = x_prompt.reshape(Bp * Tp, D_MODEL)
    xs = x_sample.reshape(Bs * Ts, D_MODEL)
    xpb, xsb = xp.astype(BF16), xs.astype(BF16)
    zero_buf = jnp.zeros((Bp, POOL_HALO, D_A), F32)
    zero_state = jnp.zeros((Bp, H_B, DK_B, DK_B), F32)
    lb_raw = hgrn_lb.astype(F32)
    w_in_t = jnp.swapaxes(w_in, 1, 2)
    w_down = w_ff_down.astype(BF16)

    outs = [[] for _ in range(10)]
    for i in range(DEPTH):
        lw = {
            "fox_fb": jnp.pad(fox_fb[i].astype(F32), (0, LANES - H_C)).reshape(1, LANES),
            "w_pool": w_pool[i].astype(BF16),
            "pool_scale": pool_scale[i].astype(F32).reshape(1, D_A),
            "hgrn_norm_g": hgrn_norm_g[i].astype(F32).reshape(1, DK_B),
            "w_up_a": w_up_a[i].astype(BF16), "w_up_b": w_up_b[i].astype(BF16), "w_up_c": w_up_c[i].astype(BF16),
        }
        vec = lambda a: a[i].astype(F32).reshape(1, D_MODEL)
        pp = p_prompt[i].reshape(Bp * Tp, -1).astype(BF16)
        ps = p_sample[i].reshape(Bs * Ts, -1).astype(BF16)

        zp, zs = _in_proj(xpb, xsb, w_in_t, i)
        merged_p, buf_p, s_p, f_p = _mixers(
            zp, lw, layer=i, B=Bp, T=Tp, pool_buf16=zero_buf, pos0=0, hgrn_s0=zero_state, lb_raw=lb_raw)
        buf16 = jnp.pad(state_pool[i], ((0, 0), (POOL_HALO - POOL_BUF, 0), (0, 0)))
        merged_s, buf_s, s_s, f_s = _mixers(
            zs, lw, layer=i, B=Bs, T=Ts, pool_buf16=buf16, pos0=past_len, hgrn_s0=state_hgrn[i], lb_raw=lb_raw,
            sample=sample_ctx)

        mix_p, mix_s = _proj(merged_p, merged_s, w_out, i)
        hp, hpb = _res_ln(xp, [mix_p], vec(ln1_g), vec(ln1_b))
        hs, hsb = _res_ln(xs, [mix_s], vec(ln1_g), vec(ln1_b))
        up_p, up_s = _proj(hpb, hsb, w_ff_up, i, out_dtype=BF16, act="relu2")
        ff_p = _mm(up_p, w_down, i)
        ff_s = _mm(up_s, w_down, i)
        ple_p, ple_s = _proj(hpb, hsb, w_ple_gate, i, act="sigmoid", side=(pp, ps, w_ple), tm=1024, tn=512)
        xp, xpb = _res_ln(hp, [ff_p, ple_p], vec(ln2_g), vec(ln2_b))
        xs, xsb = _res_ln(hs, [ff_s, ple_s], vec(ln2_g), vec(ln2_b))

        vals = (zp["k_c"].reshape(Bp, Tp, H_C, DH_C), zp["v_c"].reshape(Bp, Tp, H_C, DH_C), f_p,
                zs["k_c"].reshape(Bs, Ts, H_C, DH_C), zs["v_c"].reshape(Bs, Ts, H_C, DH_C), f_s,
                s_p, s_s, buf_p, buf_s)
        for o, val in zip(outs, vals):
            o.append(val)
    return (xp.reshape(Bp, Tp, D_MODEL), xs.reshape(Bs, Ts, D_MODEL)) + tuple(jnp.stack(o) for o in outs)
```

```python
import functools
import math

import jax
import jax.numpy as jnp
from jax import lax
from jax.experimental import pallas as pl
from jax.experimental.pallas import tpu as pltpu

F32 = jnp.float32
BF16 = jnp.bfloat16

D_MODEL = 4096
DEPTH = 2
PAGE_SIZE = 128
POOL_WINDOWS = (2, 4, 8, 16)
D_A = D_MODEL // 4
G_A = D_A // len(POOL_WINDOWS)
POOL_BUF = 15
D_B = D_MODEL // 4
DK_B = 128
H_B = D_B // DK_B
D_C = D_MODEL // 2
DH_C = 128
H_C = D_C // DH_C
D_FF = 4 * D_MODEL
DN_ALPHA = (2 * DEPTH) ** 0.25
LN_EPS = 1e-5
RMS_EPS = 1e-6

LANES = 128
SUBLANES = 8
VMEM_LIMIT = 58 * 1024 * 1024

HGRN_CHUNK = 16
POOL_HALO = 16

NEG_INF = float("-inf")


def _cparams(sem):
    return pltpu.CompilerParams(dimension_semantics=sem, vmem_limit_bytes=VMEM_LIMIT)


def _dot(a, b):
    return jnp.dot(a, b, preferred_element_type=F32)


def _dot_nt(a, b):
    return lax.dot_general(a, b, (((1,), (1,)), ((), ())), preferred_element_type=F32)


def _dot_tn(a, b):
    return lax.dot_general(a, b, (((0,), (0,)), ((), ())), preferred_element_type=F32)


def _dot01(sel, x):
    hi = x.astype(BF16)
    r1 = x - hi.astype(F32)
    mid = r1.astype(BF16)
    lo = (r1 - mid.astype(F32)).astype(BF16)
    return _dot(sel, hi) + _dot(sel, mid) + _dot(sel, lo)


def _log_sigmoid(z):
    return jnp.minimum(z, 0.0) - jnp.log1p(jnp.exp(-jnp.abs(z)))


def _sigmoid(z):
    return 1.0 / (1.0 + jnp.exp(-z))


def _mm_kernel(x_ref, w_ref, o_ref):
    @pl.when(pl.program_id(2) == 0)
    def _():
        o_ref[...] = jnp.zeros_like(o_ref)

    o_ref[...] += _dot(x_ref[...], w_ref[...])


def _mm(x, w, layer, *, tm=1024, tn=1024, tk=4096):
    M, K = x.shape
    N = w.shape[2]
    tm, tn, tk = min(tm, M), min(tn, N), min(tk, K)
    assert M % tm == 0 and N % tn == 0 and K % tk == 0
    return pl.pallas_call(
        _mm_kernel,
        grid=(M // tm, N // tn, K // tk),
        in_specs=[pl.BlockSpec((tm, tk), lambda i, j, k: (i, k)),
                  pl.BlockSpec((None, tk, tn), lambda i, j, k: (layer, k, j))],
        out_specs=pl.BlockSpec((tm, tn), lambda i, j, k: (i, j)),
        out_shape=jax.ShapeDtypeStruct((M, N), F32),
        compiler_params=_cparams(("parallel", "parallel", "arbitrary")),
        name="dense",
    )(x, w)


PROJ_TM = 1024
PROJ_TN = 1024
CAST_ROWS = 256


def _proj_kernel(*refs, act, has_side, transposed, layer, start, tn):
    refs = list(refs)
    xp_ref, xs_ref, w_hbm = refs[:3]
    del refs[:3]
    if has_side:
        pp_ref, ps_ref, w2_ref = refs[:3]
        del refs[:3]
    op_ref, os_ref, wf32_ref, wbf_ref, sem = refs[:5]
    w2bf_ref = refs[5] if has_side else None
    j, i = pl.program_id(0), pl.program_id(1)
    nj = pl.num_programs(0)
    n_rows = wbf_ref.shape[0]
    mm = _dot_nt if transposed else _dot

    def tile_copy(jj):
        cols = pl.ds(pl.multiple_of(start + jj * tn, math.gcd(start, tn)), tn)
        src = w_hbm.at[layer, cols, :] if transposed else w_hbm.at[layer, :, cols]
        return pltpu.make_async_copy(src, wf32_ref, sem)

    def finish(acc, p_ref, o_ref):
        if act == "sigmoid":
            acc = _sigmoid(acc)
        elif act == "relu2":
            acc = jnp.square(jnp.maximum(acc, 0.0))
        if has_side:
            acc = _dot(p_ref[...], w2bf_ref[...]) * acc
        o_ref[...] = acc.astype(o_ref.dtype)

    @pl.when(i == 0)
    def _():
        @pl.when(j == 0)
        def _():
            tile_copy(0).start()

        tile_copy(j).wait()
        for r in range(0, n_rows, CAST_ROWS):
            n = min(CAST_ROWS, n_rows - r)
            wbf_ref[r:r + n, :] = wf32_ref[r:r + n, :].astype(BF16)
        if has_side:
            w2bf_ref[...] = w2_ref[...].astype(BF16)
        finish(mm(xs_ref[...], wbf_ref[...]), ps_ref if has_side else None, os_ref)

    @pl.when((i == 1) & (j + 1 < nj))
    def _():
        tile_copy(j + 1).start()

    @pl.when(i > 0)
    def _():
        finish(mm(xp_ref[...], wbf_ref[...]), pp_ref if has_side else None, op_ref)


def _proj(xp, xs, w, layer, *, n_off=0, n_out=None, out_dtype=F32, act=None, side=None, transposed=False,
          tm=PROJ_TM, tn=PROJ_TN):
    Mp, K = xp.shape
    Ms = xs.shape[0]
    n_all = w.shape[1] if transposed else w.shape[2]
    n_out = n_all - n_off if n_out is None else n_out
    tm = min(tm, Mp)
    assert Mp % tm == 0 and n_out % tn == 0
    assert n_off % (SUBLANES if transposed else LANES) == 0
    nmp = Mp // tm
    row = lambda i: jnp.maximum(i - 1, 0)
    in_specs = [pl.BlockSpec((tm, K), lambda j, i: (row(i), 0)), pl.BlockSpec((Ms, K), lambda j, i: (0, 0)),
                pl.BlockSpec(memory_space=pl.ANY)]
    args = [xp, xs, w]
    w_tile = (tn, K) if transposed else (K, tn)
    scratch = [pltpu.VMEM(w_tile, F32), pltpu.VMEM(w_tile, BF16), pltpu.SemaphoreType.DMA]
    if side is not None:
        pp, ps, w2 = side
        K2 = pp.shape[1]
        in_specs += [pl.BlockSpec((tm, K2), lambda j, i: (row(i), 0)),
                     pl.BlockSpec((Ms, K2), lambda j, i: (0, 0)),
                     pl.BlockSpec((None, K2, tn), lambda j, i: (layer, 0, j))]
        args += [pp, ps, w2]
        scratch.append(pltpu.VMEM((K2, tn), BF16))
    return pl.pallas_call(
        functools.partial(_proj_kernel, act=act, has_side=side is not None, transposed=transposed, layer=layer,
                          start=n_off, tn=tn),
        grid=(n_out // tn, nmp + 1),
        in_specs=in_specs,
        out_specs=[pl.BlockSpec((tm, tn), lambda j, i: (row(i), j)),
                   pl.BlockSpec((Ms, tn), lambda j, i: (0, j))],
        out_shape=[jax.ShapeDtypeStruct((Mp, n_out), out_dtype), jax.ShapeDtypeStruct((Ms, n_out), out_dtype)],
        scratch_shapes=scratch,
        compiler_params=_cparams(("arbitrary", "arbitrary")),
        name="proj",
    )(*args)


def _pool_kernel(u_ref, buf_ref, w_ref, scale_ref, y_ref, nbuf_ref, ext_ref, *, tt, pos0):
    t = pl.program_id(1)

    @pl.when(t == 0)
    def _():
        ext_ref[0:POOL_HALO, :] = buf_ref[0]

    u = u_ref[0]
    ext_ref[POOL_HALO:POOL_HALO + tt, :] = u
    pos = pos0 + t * tt + lax.broadcasted_iota(jnp.int32, (tt, 1), 0)
    for gi, w in enumerate(POOL_WINDOWS):
        cols = slice(gi * G_A, (gi + 1) * G_A)
        win = u[:, cols]
        for j in range(1, w):
            win = win + ext_ref[POOL_HALO - j:POOL_HALO - j + tt, cols]
        cnt = jnp.minimum(pos + 1, w).astype(F32)
        pooled = win / cnt - u[:, cols]
        y = _dot(pooled.astype(BF16), w_ref[gi]) * scale_ref[:, cols]
        y_ref[0, :, cols] = y.astype(y_ref.dtype)
    tail = ext_ref[tt:tt + POOL_HALO, :]
    nbuf_ref[0] = tail
    ext_ref[0:POOL_HALO, :] = tail


def _pool_mix(u, buf16, w_pool, scale, *, pos0, tt):
    B, T, _ = u.shape
    tt = min(tt, T)
    assert T % tt == 0
    return pl.pallas_call(
        functools.partial(_pool_kernel, tt=tt, pos0=pos0),
        grid=(B, T // tt),
        in_specs=[pl.BlockSpec((1, tt, D_A), lambda b, t: (b, t, 0)),
                  pl.BlockSpec((1, POOL_HALO, D_A), lambda b, t: (b, 0, 0)),
                  pl.BlockSpec((len(POOL_WINDOWS), G_A, G_A), lambda b, t: (0, 0, 0)),
                  pl.BlockSpec((1, D_A), lambda b, t: (0, 0))],
        out_specs=[pl.BlockSpec((1, tt, D_A), lambda b, t: (b, t, 0)),
                   pl.BlockSpec((1, POOL_HALO, D_A), lambda b, t: (b, 0, 0))],
        out_shape=[jax.ShapeDtypeStruct((B, T, D_A), BF16),
                   jax.ShapeDtypeStruct((B, POOL_HALO, D_A), F32)],
        scratch_shapes=[pltpu.VMEM((POOL_HALO + tt, D_A), F32)],
        compiler_params=_cparams(("parallel", "arbitrary")),
        name="pool_mix",
    )(u, buf16, w_pool, scale)


def _hgrn_kernel(q_ref, f_ref, i_ref, g_ref, lbraw_ref, s0_ref, ng_ref, y_ref, sfin_ref,
                 st_ref, b_ref, kk_ref, v_ref, qd_ref, o_ref, *, layer, tt, t_valid):
    t = pl.program_id(1)
    nt = pl.num_programs(1)
    C = HGRN_CHUNK

    @pl.when(t == 0)
    def _():
        for h in range(H_B):
            st_ref[h] = s0_ref[0, h].T

    raw = lbraw_ref[...]
    e = jnp.exp(raw - jnp.max(raw, axis=0, keepdims=True))
    sm = e / jnp.sum(e, axis=0, keepdims=True)
    lb = jnp.zeros((1, D_B), F32)
    for j in range(1, layer + 1):
        lb = lb + sm[j:j + 1]
    lb = (lb + sm[0:1]) - sm[0:1]

    z = f_ref[0]
    la = jnp.log(lb)
    c = jnp.log1p(-lb) + _log_sigmoid(z)
    amax = jnp.maximum(la, c)
    delta = la - c
    lf = jnp.where(jnp.isnan(delta), la + c, amax + jnp.log1p(jnp.exp(-jnp.abs(delta))))
    kk = (1.0 - lb) * _sigmoid(-z)
    if t_valid is not None:
        row = t * tt + lax.broadcasted_iota(jnp.int32, (tt, 1), 0)
        lf = jnp.where(row < t_valid, lf, 0.0)
        kk = jnp.where(row < t_valid, kk, 0.0)
    r = lax.broadcasted_iota(jnp.int32, (tt, tt), 0)
    s = lax.broadcasted_iota(jnp.int32, (tt, tt), 1)
    tri = jnp.where((s <= r) & (s >= jnp.bitwise_and(r, -C)), 1.0, 0.0).astype(BF16)
    b_all = _dot01(tri, lf)
    qs_all = q_ref[0] * DK_B ** -0.5
    v_all = i_ref[0]
    heads = [slice(h * DK_B, (h + 1) * DK_B) for h in range(H_B)]
    halo = jnp.zeros((C, DK_B), F32)
    for h, hl in enumerate(heads):
        for ref, val in ((b_ref, b_all), (kk_ref, kk), (v_ref, v_all)):
            ref[h, 0:C, :] = halo
            ref[h, C:C + tt, :] = val[:, hl]

    tmod = jnp.bitwise_and(lax.broadcasted_iota(jnp.int32, (tt, 1), 0), C - 1)
    for h, hl in enumerate(heads):
        bh, qh = b_all[:, hl], qs_all[:, hl]
        acc = jnp.zeros((tt, DK_B), F32)
        for d in range(C):
            lag = slice(C - d, C - d + tt)
            w = qh * jnp.exp(jnp.minimum(bh - b_ref[h, lag, :], 0.0)) * kk_ref[h, lag, :]
            rs = jnp.sum(w, axis=-1, keepdims=True)
            acc = acc + jnp.where(tmod >= d, rs, 0.0) * v_ref[h, lag, :]
        o_ref[:, hl] = acc
        qd_ref[:, hl] = qh * jnp.exp(bh)

    def chunk(ci, carry):
        rows = pl.ds(pl.multiple_of(ci * C, C), C)
        rows_h = pl.ds(pl.multiple_of(ci * C + C, C), C)
        for h, hl in enumerate(heads):
            b = b_ref[h, rows_h, :]
            v = v_ref[h, rows_h, :]
            st = st_ref[h]
            o_ref[rows, hl] += _dot_nt(qd_ref[rows, hl].astype(BF16), st.astype(BF16))
            b_last = b[C - 1:C]
            k_dec = kk_ref[h, rows_h, :] * jnp.exp(b_last - b)
            st_ref[h] = st * jnp.exp(b_last) + _dot_tn(v.astype(BF16), k_dec.astype(BF16))
        return carry

    n_chunks = tt // C
    lax.fori_loop(0, n_chunks, chunk, 0, unroll=math.gcd(n_chunks, 4))

    for h, hl in enumerate(heads):
        o = o_ref[:, hl]
        o = o * lax.rsqrt(jnp.mean(jnp.square(o), axis=-1, keepdims=True) + RMS_EPS) * ng_ref[...]
        o = o * _sigmoid(g_ref[0, :, hl])
        y_ref[0, :, hl] = o.astype(y_ref.dtype)

    @pl.when(t == nt - 1)
    def _():
        for h in range(H_B):
            sfin_ref[0, h] = st_ref[h].T


def _hgrn2(zb, lb_raw, s0, norm_g, *, layer, tt, t_valid=None):
    B, T, _ = zb.shape
    tt = min(tt, T)
    assert T % tt == 0 and tt % HGRN_CHUNK == 0

    def col(j):
        return pl.BlockSpec((1, tt, D_B), lambda b, t: (b, t, j))

    return pl.pallas_call(
        functools.partial(_hgrn_kernel, layer=layer, tt=tt, t_valid=t_valid),
        grid=(B, T // tt),
        in_specs=[col(0), col(1), col(2), col(3),
                  pl.BlockSpec((DEPTH, D_B), lambda b, t: (0, 0)),
                  pl.BlockSpec((1, H_B, DK_B, DK_B), lambda b, t: (b, 0, 0, 0)),
                  pl.BlockSpec((1, DK_B), lambda b, t: (0, 0))],
        out_specs=[pl.BlockSpec((1, tt, D_B), lambda b, t: (b, t, 0)),
                   pl.BlockSpec((1, H_B, DK_B, DK_B), lambda b, t: (b, 0, 0, 0))],
        out_shape=[jax.ShapeDtypeStruct((B, T, D_B), BF16),
                   jax.ShapeDtypeStruct((B, H_B, DK_B, DK_B), F32)],
        scratch_shapes=[pltpu.VMEM((H_B, DK_B, DK_B), F32)]
        + [pltpu.VMEM((H_B, HGRN_CHUNK + tt, DK_B), F32)] * 3
        + [pltpu.VMEM((tt, D_B), F32)] * 2,
        compiler_params=_cparams(("parallel", "arbitrary")),
        name="hgrn2",
    )(zb, zb, zb, zb, lb_raw, s0, norm_g)


def _fox_gate_kernel(f_ref, fb_ref, logf_ref, c_ref, *, t_valid):
    S = f_ref.shape[1]
    logf = _log_sigmoid(f_ref[0] + fb_ref[...])
    logf_ref[0] = logf
    if t_valid is not None:
        row = lax.broadcasted_iota(jnp.int32, (S, 1), 0)
        logf = jnp.where(row < t_valid, logf, 0.0)
    r = lax.broadcasted_iota(jnp.int32, (S, S), 0)
    s = lax.broadcasted_iota(jnp.int32, (S, S), 1)
    tri = jnp.where(s <= r, 1.0, 0.0).astype(BF16)
    c_ref[0] = _dot01(tri, logf)


def _fox_gate(f_pad, fb_pad, *, t_valid=None):
    B, S, L = f_pad.shape
    spec = pl.BlockSpec((1, S, L), lambda b: (b, 0, 0))
    return pl.pallas_call(
        functools.partial(_fox_gate_kernel, t_valid=t_valid),
        grid=(B,),
        in_specs=[spec, pl.BlockSpec((1, L), lambda b: (0, 0))],
        out_specs=[spec, spec],
        out_shape=[jax.ShapeDtypeStruct((B, S, L), F32)] * 2,
        compiler_params=_cparams(("parallel",)),
        name="fox_gate",
    )(f_pad, fb_pad)


def _fox_prompt_kernel(q_ref, k_ref, v_ref, ccol_ref, crow_ref, o_ref, *, tq):
    h = pl.program_id(1)
    qi = pl.program_id(2)
    q = q_ref[0]
    lane = lax.broadcasted_iota(jnp.int32, (tq, LANES), 1)
    cq = jnp.sum(jnp.where(lane == h, ccol_ref[0], 0.0), axis=-1, keepdims=True)
    scale = DH_C ** -0.5

    def block(j, carry, masked):
        m, l, acc = carry
        ks = pl.ds(pl.multiple_of(j * tq, tq), tq)
        kb = k_ref[0, ks, :].astype(BF16)
        vb = v_ref[0, ks, :].astype(BF16)
        s = _dot_nt(q, kb) * scale
        s = s + (cq - crow_ref[0, 0, :, ks])
        if masked:
            rr = lax.broadcasted_iota(jnp.int32, (tq, tq), 0)
            cc = lax.broadcasted_iota(jnp.int32, (tq, tq), 1)
            s = jnp.where(cc <= rr, s, NEG_INF)
        m_new = jnp.maximum(m, jnp.max(s, axis=-1, keepdims=True))
        alpha = jnp.exp(m - m_new)
        p = jnp.exp(s - m_new)
        l = alpha * l + jnp.sum(p, axis=-1, keepdims=True)
        acc = alpha * acc + _dot(p.astype(BF16), vb)
        return m_new, l, acc

    init = (jnp.full((tq, 1), NEG_INF, F32), jnp.zeros((tq, 1), F32), jnp.zeros((tq, DH_C), F32))
    carry = block(qi, init, True)
    m, l, acc = lax.fori_loop(0, qi, lambda j, cr: block(j, cr, False), carry)
    o_ref[0] = (acc / l).astype(o_ref.dtype)


def _fox_prompt(q, k, v, c_col, c_row, *, tq=512):
    B, S, _ = q.shape
    tq = min(tq, S)
    assert S % tq == 0
    kv_spec = pl.BlockSpec((1, S, DH_C), lambda b, h, i: (b, 0, h))
    return pl.pallas_call(
        functools.partial(_fox_prompt_kernel, tq=tq),
        grid=(B, H_C, S // tq),
        in_specs=[pl.BlockSpec((1, tq, DH_C), lambda b, h, i: (b, i, h)), kv_spec, kv_spec,
                  pl.BlockSpec((1, tq, LANES), lambda b, h, i: (b, i, 0)),
                  pl.BlockSpec((1, 1, 1, S), lambda b, h, i: (b, h, 0, 0))],
        out_specs=pl.BlockSpec((1, tq, DH_C), lambda b, h, i: (b, i, h)),
        out_shape=jax.ShapeDtypeStruct((B, S, D_C), BF16),
        compiler_params=_cparams(("parallel", "parallel", "arbitrary")),
        name="fox_prompt",
    )(q, k, v, c_col, c_row)


def _fox_suffix_kernel(pt_ref, lf_ref, usel_ref, tsel_ref, esel_ref, o_ref, g_ref, *, n_pages):
    b = pl.program_id(0)
    for p in range(n_pages):
        g_ref[p:p + 1, :] = lf_ref[pl.ds(pt_ref[b, p], 1), :]
    g = g_ref[...]
    within = _dot01_rhs(g, usel_ref[...])
    tot = _dot01_rhs(g, tsel_ref[...])
    r = lax.broadcasted_iota(jnp.int32, (n_pages, n_pages), 0)
    s = lax.broadcasted_iota(jnp.int32, (n_pages, n_pages), 1)
    later = jnp.where(s > r, 1.0, 0.0).astype(BF16)
    carry = _dot01(later, tot)
    suf = within + _dot01_rhs(carry, esel_ref[...])
    for h in range(H_C):
        o_ref[0, h] = suf[:, h * PAGE_SIZE:(h + 1) * PAGE_SIZE]


def _dot01_rhs(x, sel):
    hi = x.astype(BF16)
    r1 = x - hi.astype(F32)
    mid = r1.astype(BF16)
    lo = (r1 - mid.astype(F32)).astype(BF16)
    return _dot(hi, sel) + _dot(mid, sel) + _dot(lo, sel)


def _fox_suffix(page_table, cache_logf_l):
    B, n_pages = page_table.shape
    n_pool, W = cache_logf_l.shape
    key_in = jnp.arange(W) // H_C
    head_in = jnp.arange(W) % H_C
    head_out = jnp.arange(W) // PAGE_SIZE
    key_out = jnp.arange(W) % PAGE_SIZE
    usel = ((head_in[:, None] == head_out[None, :]) & (key_in[:, None] > key_out[None, :])).astype(BF16)
    tsel = (head_in[:, None] == jnp.arange(LANES)[None, :]).astype(BF16)
    esel = (jnp.arange(LANES)[:, None] == head_out[None, :]).astype(BF16)
    full = lambda shp: pl.BlockSpec(shp, lambda b, pt: tuple(0 for _ in shp))
    grid_spec = pltpu.PrefetchScalarGridSpec(
        num_scalar_prefetch=1,
        grid=(B,),
        in_specs=[full((n_pool, W)), full((W, W)), full((W, LANES)), full((LANES, W))],
        out_specs=pl.BlockSpec((1, H_C, n_pages, PAGE_SIZE), lambda b, pt: (b, 0, 0, 0)),
        scratch_shapes=[pltpu.VMEM((n_pages, W), F32)],
    )
    return pl.pallas_call(
        functools.partial(_fox_suffix_kernel, n_pages=n_pages),
        grid_spec=grid_spec,
        out_shape=jax.ShapeDtypeStruct((B, H_C, n_pages, PAGE_SIZE), F32),
        compiler_params=_cparams(("arbitrary",)),
        name="fox_suffix",
    )(page_table, cache_logf_l, usel, tsel, esel)


def _fox_sample_kernel(pt_ref, q_ref, kn_ref, vn_ref, cq_ref, ckn_ref, suf_ref, *rest, t_new, ppg):
    k_refs = rest[:ppg]
    v_refs = rest[ppg:2 * ppg]
    o_ref, m_ref, l_ref, acc_ref = rest[2 * ppg:]
    g = pl.program_id(1)
    ng = pl.num_programs(1)
    scale = DH_C ** -0.5
    R = H_C * t_new

    def q_head(h):
        return q_ref[0, :, h * DH_C:(h + 1) * DH_C]

    def update(s, v_of_head):
        m = m_ref[...]
        m_new = jnp.maximum(m, jnp.max(s, axis=-1, keepdims=True))
        alpha = jnp.exp(m - m_new)
        p = jnp.exp(s - m_new)
        l_ref[...] = alpha * l_ref[...] + jnp.sum(p, axis=-1, keepdims=True)
        m_ref[...] = m_new
        for h in range(H_C):
            rows = slice(h * t_new, (h + 1) * t_new)
            acc_ref[rows, :] = alpha[rows] * acc_ref[rows, :] + _dot(p[rows].astype(BF16), v_of_head(h))

    @pl.when(g == 0)
    def _():
        m_ref[...] = jnp.full_like(m_ref, NEG_INF)
        l_ref[...] = jnp.zeros_like(l_ref)
        acc_ref[...] = jnp.zeros_like(acc_ref)
        s = jnp.concatenate(
            [_dot_nt(q_head(h), kn_ref[0, :, h * DH_C:(h + 1) * DH_C]) for h in range(H_C)], axis=0) * scale
        s = s + (cq_ref[0] - ckn_ref[0])
        qpos = lax.broadcasted_iota(jnp.int32, s.shape, 0) % t_new
        kpos = lax.broadcasted_iota(jnp.int32, s.shape, 1)
        s = jnp.where(kpos <= qpos, s, NEG_INF)
        update(s, lambda h: vn_ref[0, :, h * DH_C:(h + 1) * DH_C])

    def head_rows(ref, h):
        return ref[0, 0, pl.ds(h, PAGE_SIZE, stride=H_C), :].astype(BF16)

    for j in range(ppg):
        kj, vj = k_refs[j], v_refs[j]
        s = jnp.concatenate([_dot_nt(q_head(h), head_rows(kj, h)) for h in range(H_C)], axis=0) * scale
        bias = jnp.concatenate(
            [jnp.broadcast_to(suf_ref[0, h, pl.ds(g * ppg + j, 1), :], (t_new, PAGE_SIZE)) for h in range(H_C)], axis=0)
        s = s + (cq_ref[0] + bias)
        update(s, lambda h: head_rows(vj, h))

    @pl.when(g == ng - 1)
    def _():
        o = acc_ref[...] / l_ref[...]
        for h in range(H_C):
            o_ref[0, :, h * DH_C:(h + 1) * DH_C] = o[h * t_new:(h + 1) * t_new].astype(o_ref.dtype)


def _fox_sample(q, k_new, v_new, cq_col, ck_new, suffix, cache_k, cache_v, page_table, *, layer, t_new, ppg=8):
    B, n_pages = page_table.shape
    T = q.shape[1]
    T16 = k_new.shape[1]
    R = H_C * t_new
    assert n_pages % ppg == 0

    def page_spec(j):
        return pl.BlockSpec((1, 1, PAGE_SIZE * H_C, DH_C),
                            lambda b, g, pt: (layer, pt[b, g * ppg + j], 0, 0))

    bspec = lambda shp: pl.BlockSpec((1,) + shp, lambda b, g, pt: (b,) + tuple(0 for _ in shp))
    grid_spec = pltpu.PrefetchScalarGridSpec(
        num_scalar_prefetch=1,
        grid=(B, n_pages // ppg),
        in_specs=[bspec((T, D_C)), bspec((T16, D_C)), bspec((T16, D_C)), bspec((R, 1)), bspec((R, T16)),
                  bspec((H_C, n_pages, PAGE_SIZE))]
        + [page_spec(j) for j in range(ppg)] + [page_spec(j) for j in range(ppg)],
        out_specs=bspec((T, D_C)),
        scratch_shapes=[pltpu.VMEM((R, 1), F32), pltpu.VMEM((R, 1), F32), pltpu.VMEM((R, DH_C), F32)],
    )
    return pl.pallas_call(
        functools.partial(_fox_sample_kernel, t_new=t_new, ppg=ppg),
        grid_spec=grid_spec,
        out_shape=jax.ShapeDtypeStruct((B, T, D_C), BF16),
        compiler_params=_cparams(("parallel", "arbitrary")),
        name="fox_sample",
    )(page_table, q, k_new, v_new, cq_col, ck_new, suffix, *([cache_k] * ppg), *([cache_v] * ppg))


def _merge_kernel(ya_ref, yb_ref, yc_ref, wa_ref, wb_ref, wc_ref, ga_ref, gb_ref, gc_ref, o_ref):
    m = ga_ref[...].astype(F32) * _dot(ya_ref[...], wa_ref[...])
    m = m + gb_ref[...].astype(F32) * _dot(yb_ref[...], wb_ref[...])
    m = m + gc_ref[...].astype(F32) * _dot(yc_ref[...], wc_ref[...])
    o_ref[...] = m.astype(o_ref.dtype)


def _merge(y_a, y_b, y_c, w_a, w_b, w_c, gates, *, tm=1024, tn=512):
    M = y_a.shape[0]
    tm = min(tm, M)
    nj = D_MODEL // tn
    act = lambda K: pl.BlockSpec((tm, K), lambda i, j: (i, 0))
    wsp = lambda K: pl.BlockSpec((K, tn), lambda i, j: (0, j))
    gsp = lambda o: pl.BlockSpec((tm, tn), lambda i, j: (i, j + o * nj))
    return pl.pallas_call(
        _merge_kernel,
        grid=(M // tm, nj),
        in_specs=[act(D_A), act(D_B), act(D_C), wsp(D_A), wsp(D_B), wsp(D_C), gsp(0), gsp(1), gsp(2)],
        out_specs=pl.BlockSpec((tm, tn), lambda i, j: (i, j)),
        out_shape=jax.ShapeDtypeStruct((M, D_MODEL), BF16),
        compiler_params=_cparams(("parallel", "parallel")),
        name="merge",
    )(y_a, y_b, y_c, w_a, w_b, w_c, gates, gates, gates)


def _ln_kernel(*refs, n_delta):
    x_ref = refs[0]
    d_refs = refs[1:1 + n_delta]
    g_ref, b_ref, o_ref, ob_ref = refs[1 + n_delta:]
    y = DN_ALPHA * x_ref[...]
    for d in d_refs:
        y = y + d[...].astype(F32)
    mu = jnp.mean(y, axis=-1, keepdims=True)
    yc = y - mu
    var = jnp.mean(jnp.square(yc), axis=-1, keepdims=True)
    o = yc * lax.rsqrt(var + LN_EPS) * g_ref[...] + b_ref[...]
    o_ref[...] = o
    ob_ref[...] = o.astype(BF16)


def _res_ln(x, deltas, g, b, *, tm=128):
    M = x.shape[0]
    tm = min(tm, M)
    row = pl.BlockSpec((tm, D_MODEL), lambda i: (i, 0))
    vec = pl.BlockSpec((1, D_MODEL), lambda i: (0, 0))
    return pl.pallas_call(
        functools.partial(_ln_kernel, n_delta=len(deltas)),
        grid=(M // tm,),
        in_specs=[row] * (1 + len(deltas)) + [vec, vec],
        out_specs=[row, row],
        out_shape=[jax.ShapeDtypeStruct((M, D_MODEL), F32), jax.ShapeDtypeStruct((M, D_MODEL), BF16)],
        compiler_params=_cparams(("parallel",)),
        name="res_ln",
    )(x, *deltas, g, b)


OFF_A = 0
OFF_B = D_A
OFF_QC = OFF_B + 4 * D_B
OFF_KC = OFF_QC + D_C
OFF_VC = OFF_KC + D_C
OFF_FC = OFF_VC + D_C
N_GATES = 3 * D_MODEL


def _in_proj(xpb, xsb, w_in_t, layer):
    seg = lambda **kw: _proj(xpb, xsb, w_in_t, layer, transposed=True, **kw)
    parts = {
        "u_a": seg(n_off=OFF_A, n_out=D_A),
        "zb": seg(n_off=OFF_B, n_out=4 * D_B),
        "q_c": seg(n_off=OFF_QC, n_out=D_C, out_dtype=BF16),
        "k_c": seg(n_off=OFF_KC, n_out=D_C),
        "v_c": seg(n_off=OFF_VC, n_out=D_C),
        "f_c": seg(n_off=OFF_FC, n_out=LANES, tn=LANES),
        "gates": seg(n_off=OFF_FC + H_C, n_out=N_GATES, out_dtype=BF16, act="sigmoid"),
    }
    return ({k: v[0] for k, v in parts.items()}, {k: v[1] for k, v in parts.items()})


def _mixers(z, lw, *, layer, B, T, pool_buf16, pos0, hgrn_s0, lb_raw, sample=None):
    M = B * T
    tm = min(PROJ_TM, M)
    u_a, zb, q_c, k_c, v_c, f_c, gates = (z[k] for k in ("u_a", "zb", "q_c", "k_c", "v_c", "f_c", "gates"))

    y_a, nbuf = _pool_mix(u_a.reshape(B, T, D_A), pool_buf16, lw["w_pool"], lw["pool_scale"], pos0=pos0, tt=256)
    new_buf = nbuf[:, 1:]

    zb3 = zb.reshape(B, T, 4 * D_B)
    if T % HGRN_CHUNK:
        Tp = -(-T // HGRN_CHUNK) * HGRN_CHUNK
        zb3 = jnp.pad(zb3, ((0, 0), (0, Tp - T), (0, 0)))
        y_b, new_s = _hgrn2(zb3, lb_raw, hgrn_s0, lw["hgrn_norm_g"], layer=layer, tt=128, t_valid=T)
        y_b = y_b[:, :T]
    else:
        y_b, new_s = _hgrn2(zb3, lb_raw, hgrn_s0, lw["hgrn_norm_g"], layer=layer, tt=128)

    q3, k3, v3 = (a.reshape(B, T, D_C) for a in (q_c, k_c, v_c))
    if sample is None:
        logf, c = _fox_gate(f_c.reshape(B, T, LANES), lw["fox_fb"])
        c_row = jnp.transpose(c[:, :, :H_C], (0, 2, 1)).reshape(B, H_C, 1, T)
        y_c = _fox_prompt(q3, k3, v3, c, c_row)
    else:
        T16 = PAGE_SIZE
        f3 = jnp.pad(f_c.reshape(B, T, LANES), ((0, 0), (0, T16 - T), (0, 0)))
        logf, c = _fox_gate(f3, lw["fox_fb"], t_valid=T)
        logf = logf[:, :T]
        c_hm = jnp.transpose(c[:, :, :H_C], (0, 2, 1))
        cq_col = c_hm[:, :, :T].reshape(B, H_C * T, 1)
        ck_new = jnp.repeat(c_hm, T, axis=1)
        pad_kv = lambda a: jnp.pad(a.astype(BF16), ((0, 0), (0, T16 - T), (0, 0)))
        suffix = _fox_suffix(sample["page_table"], sample["cache_logf"][layer])
        y_c = _fox_sample(q3, pad_kv(k3), pad_kv(v3), cq_col, ck_new, suffix, sample["cache_k"], sample["cache_v"],
                          sample["page_table"], layer=layer, t_new=T)
    logf = logf[:, :, :H_C]

    merged = _merge(y_a.reshape(M, D_A), y_b.reshape(M, D_B), y_c.reshape(M, D_C),
                    lw["w_up_a"], lw["w_up_b"], lw["w_up_c"], gates, tm=tm)
    return merged, new_buf, new_s, logf


def kernel(x_prompt, x_sample, cache_k, cache_v, cache_logf, state_hgrn, state_pool, page_table, p_prompt, p_sample,
           w_in, fox_fb, w_pool, pool_scale, hgrn_lb, hgrn_norm_g, w_up_a, w_up_b, w_up_c, w_out, ln1_g, ln1_b,
           w_ff_up, w_ff_down, w_ple, w_ple_gate, ln2_g, ln2_b):
    Bp, Tp, _ = x_prompt.shape
    Bs, Ts, _ = x_sample.shape
    n_pool = cache_k.shape[1]
    past_len = page_table.shape[1] * PAGE_SIZE
    cache_logf2 = cache_logf.reshape(DEPTH, n_pool, PAGE_SIZE * H_C)
    page_rows = (DEPTH, n_pool, PAGE_SIZE * H_C, DH_C)
    sample_ctx = {"page_table": page_table, "cache_k": cache_k.reshape(page_rows),
                  "cache_v": cache_v.reshape(page_rows), "cache_logf": cache_logf2}

    xp = x_prompt.reshape(Bp * Tp, D_MODEL)
    xs = x_sample.reshape(Bs * Ts, D_MODEL)
    xpb, xsb = xp.astype(BF16), xs.astype(BF16)
    zero_buf = jnp.zeros((Bp, POOL_HALO, D_A), F32)
    zero_state = jnp.zeros((Bp, H_B, DK_B, DK_B), F32)
    lb_raw = hgrn_lb.astype(F32)
    w_in_t = jnp.swapaxes(w_in, 1, 2)
    w_down = w_ff_down.astype(BF16)

    outs = [[] for _ in range(10)]
    for i in range(DEPTH):
        lw = {
            "fox_fb": jnp.pad(fox_fb[i].astype(F32), (0, LANES - H_C)).reshape(1, LANES),
            "w_pool": w_pool[i].astype(BF16),
            "pool_scale": pool_scale[i].astype(F32).reshape(1, D_A),
            "hgrn_norm_g": hgrn_norm_g[i].astype(F32).reshape(1, DK_B),
            "w_up_a": w_up_a[i].astype(BF16), "w_up_b": w_up_b[i].astype(BF16), "w_up_c": w_up_c[i].astype(BF16),
        }
        vec = lambda a: a[i].astype(F32).reshape(1, D_MODEL)
        pp = p_prompt[i].reshape(Bp * Tp, -1).astype(BF16)
        ps = p_sample[i].reshape(Bs * Ts, -1).astype(BF16)

        zp, zs = _in_proj(xpb, xsb, w_in_t, i)
        merged_p, buf_p, s_p, f_p = _mixers(
            zp, lw, layer=i, B=Bp, T=Tp, pool_buf16=zero_buf, pos0=0, hgrn_s0=zero_state, lb_raw=lb_raw)
        buf16 = jnp.pad(state_pool[i], ((0, 0), (POOL_HALO - POOL_BUF, 0), (0, 0)))
        merged_s, buf_s, s_s, f_s = _mixers(
            zs, lw, layer=i, B=Bs, T=Ts, pool_buf16=buf16, pos0=past_len, hgrn_s0=state_hgrn[i], lb_raw=lb_raw,
            sample=sample_ctx)

        mix_p, mix_s = _proj(merged_p, merged_s, w_out, i)
        hp, hpb = _res_ln(xp, [mix_p], vec(ln1_g), vec(ln1_b))
        hs, hsb = _res_ln(xs, [mix_s], vec(ln1_g), vec(ln1_b))
        up_p, up_s = _proj(hpb, hsb, w_ff_up, i, out_dtype=BF16, act="relu2")
        ff_p = _mm(up_p, w_down, i)
        ff_s = _mm(up_s, w_down, i)
        ple_p, ple_s = _proj(hpb, hsb, w_ple_gate, i, act="sigmoid", side=(pp, ps, w_ple), tm=PROJ_TM // 2)
        xp, xpb = _res_ln(hp, [ff_p, ple_p], vec(ln2_g), vec(ln2_b))
        xs, xsb = _res_ln(hs, [ff_s, ple_s], vec(ln2_g), vec(ln2_b))

        vals = (zp["k_c"].reshape(Bp, Tp, H_C, DH_C), zp["v_c"].reshape(Bp, Tp, H_C, DH_C), f_p,
                zs["k_c"].reshape(Bs, Ts, H_C, DH_C), zs["v_c"].reshape(Bs, Ts, H_C, DH_C), f_s,
                s_p, s_s, buf_p, buf_s)
        for o, val in zip(outs, vals):
            o.append(val)
    return (xp.reshape(Bp, Tp, D_MODEL), xs.reshape(Bs, Ts, D_MODEL)) + tuple(jnp.stack(o) for o in outs)
```

```python
import functools
import math

import jax
import jax.numpy as jnp
from jax import lax
from jax.experimental import pallas as pl
from jax.experimental.pallas import tpu as pltpu

F32 = jnp.float32
BF16 = jnp.bfloat16

D_MODEL = 4096
DEPTH = 2
PAGE_SIZE = 128
POOL_WINDOWS = (2, 4, 8, 16)
D_A = D_MODEL // 4
G_A = D_A // len(POOL_WINDOWS)
POOL_BUF = 15
D_B = D_MODEL // 4
DK_B = 128
H_B = D_B // DK_B
D_C = D_MODEL // 2
DH_C = 128
H_C = D_C // DH_C
D_FF = 4 * D_MODEL
DN_ALPHA = (2 * DEPTH) ** 0.25
LN_EPS = 1e-5
RMS_EPS = 1e-6

LANES = 128
SUBLANES = 8
VMEM_LIMIT = 58 * 1024 * 1024

HGRN_CHUNK = 16
POOL_HALO = 16

NEG_INF = float("-inf")


def _cparams(sem):
    return pltpu.CompilerParams(dimension_semantics=sem, vmem_limit_bytes=VMEM_LIMIT)


def _dot(a, b):
    return jnp.dot(a, b, preferred_element_type=F32)


def _dot_nt(a, b):
    return lax.dot_general(a, b, (((1,), (1,)), ((), ())), preferred_element_type=F32)


def _dot_tn(a, b):
    return lax.dot_general(a, b, (((0,), (0,)), ((), ())), preferred_element_type=F32)


def _dot01(sel, x):
    hi = x.astype(BF16)
    r1 = x - hi.astype(F32)
    mid = r1.astype(BF16)
    lo = (r1 - mid.astype(F32)).astype(BF16)
    return _dot(sel, hi) + _dot(sel, mid) + _dot(sel, lo)


def _log_sigmoid(z):
    return jnp.minimum(z, 0.0) - jnp.log1p(jnp.exp(-jnp.abs(z)))


def _sigmoid(z):
    return 1.0 / (1.0 + jnp.exp(-z))


def _mm_kernel(x_ref, w_ref, o_ref):
    @pl.when(pl.program_id(2) == 0)
    def _():
        o_ref[...] = jnp.zeros_like(o_ref)

    o_ref[...] += _dot(x_ref[...], w_ref[...])


def _mm(x, w, layer, *, tm=1024, tn=1024, tk=4096):
    M, K = x.shape
    N = w.shape[2]
    tm, tn, tk = min(tm, M), min(tn, N), min(tk, K)
    assert M % tm == 0 and N % tn == 0 and K % tk == 0
    return pl.pallas_call(
        _mm_kernel,
        grid=(M // tm, N // tn, K // tk),
        in_specs=[pl.BlockSpec((tm, tk), lambda i, j, k: (i, k)),
                  pl.BlockSpec((None, tk, tn), lambda i, j, k: (layer, k, j))],
        out_specs=pl.BlockSpec((tm, tn), lambda i, j, k: (i, j)),
        out_shape=jax.ShapeDtypeStruct((M, N), F32),
        compiler_params=_cparams(("parallel", "parallel", "arbitrary")),
        name="dense",
    )(x, w)


PROJ_TM = 1024
PROJ_TN = 1024
CAST_ROWS = 256


def _proj_kernel(*refs, act, has_side, transposed, layer, start, tn):
    refs = list(refs)
    xp_ref, xs_ref, w_hbm = refs[:3]
    del refs[:3]
    if has_side:
        pp_ref, ps_ref, w2_ref = refs[:3]
        del refs[:3]
    op_ref, os_ref, wf32_ref, wbf_ref, sem = refs[:5]
    w2bf_ref = refs[5] if has_side else None
    j, i = pl.program_id(0), pl.program_id(1)
    nj = pl.num_programs(0)
    n_rows = wbf_ref.shape[0]
    mm = _dot_nt if transposed else _dot

    def tile_copy(jj):
        cols = pl.ds(pl.multiple_of(start + jj * tn, math.gcd(start, tn)), tn)
        src = w_hbm.at[layer, cols, :] if transposed else w_hbm.at[layer, :, cols]
        return pltpu.make_async_copy(src, wf32_ref, sem)

    def finish(acc, p_ref, o_ref):
        if act == "sigmoid":
            acc = _sigmoid(acc)
        elif act == "relu2":
            acc = jnp.square(jnp.maximum(acc, 0.0))
        if has_side:
            acc = _dot(p_ref[...], w2bf_ref[...]) * acc
        o_ref[...] = acc.astype(o_ref.dtype)

    @pl.when(i == 0)
    def _():
        @pl.when(j == 0)
        def _():
            tile_copy(0).start()

        tile_copy(j).wait()
        for r in range(0, n_rows, CAST_ROWS):
            n = min(CAST_ROWS, n_rows - r)
            wbf_ref[r:r + n, :] = wf32_ref[r:r + n, :].astype(BF16)
        if has_side:
            w2bf_ref[...] = w2_ref[...].astype(BF16)
        finish(mm(xs_ref[...], wbf_ref[...]), ps_ref if has_side else None, os_ref)

    @pl.when((i == 1) & (j + 1 < nj))
    def _():
        tile_copy(j + 1).start()

    @pl.when(i > 0)
    def _():
        finish(mm(xp_ref[...], wbf_ref[...]), pp_ref if has_side else None, op_ref)


def _proj(xp, xs, w, layer, *, n_off=0, n_out=None, out_dtype=F32, act=None, side=None, transposed=False,
          tm=PROJ_TM, tn=PROJ_TN):
    Mp, K = xp.shape
    Ms = xs.shape[0]
    n_all = w.shape[1] if transposed else w.shape[2]
    n_out = n_all - n_off if n_out is None else n_out
    tm = min(tm, Mp)
    assert Mp % tm == 0 and n_out % tn == 0
    assert n_off % (SUBLANES if transposed else LANES) == 0
    nmp = Mp // tm
    row = lambda i: jnp.maximum(i - 1, 0)
    in_specs = [pl.BlockSpec((tm, K), lambda j, i: (row(i), 0)), pl.BlockSpec((Ms, K), lambda j, i: (0, 0)),
                pl.BlockSpec(memory_space=pl.ANY)]
    args = [xp, xs, w]
    w_tile = (tn, K) if transposed else (K, tn)
    scratch = [pltpu.VMEM(w_tile, F32), pltpu.VMEM(w_tile, BF16), pltpu.SemaphoreType.DMA]
    if side is not None:
        pp, ps, w2 = side
        K2 = pp.shape[1]
        in_specs += [pl.BlockSpec((tm, K2), lambda j, i: (row(i), 0)),
                     pl.BlockSpec((Ms, K2), lambda j, i: (0, 0)),
                     pl.BlockSpec((None, K2, tn), lambda j, i: (layer, 0, j))]
        args += [pp, ps, w2]
        scratch.append(pltpu.VMEM((K2, tn), BF16))
    return pl.pallas_call(
        functools.partial(_proj_kernel, act=act, has_side=side is not None, transposed=transposed, layer=layer,
                          start=n_off, tn=tn),
        grid=(n_out // tn, nmp + 1),
        in_specs=in_specs,
        out_specs=[pl.BlockSpec((tm, tn), lambda j, i: (row(i), j)),
                   pl.BlockSpec((Ms, tn), lambda j, i: (0, j))],
        out_shape=[jax.ShapeDtypeStruct((Mp, n_out), out_dtype), jax.ShapeDtypeStruct((Ms, n_out), out_dtype)],
        scratch_shapes=scratch,
        compiler_params=_cparams(("arbitrary", "arbitrary")),
        name="proj",
    )(*args)


def _pool_kernel(u_ref, buf_ref, w_ref, scale_ref, y_ref, nbuf_ref, ext_ref, *, tt, pos0):
    t = pl.program_id(1)

    @pl.when(t == 0)
    def _():
        ext_ref[0:POOL_HALO, :] = buf_ref[0]

    u = u_ref[0]
    ext_ref[POOL_HALO:POOL_HALO + tt, :] = u
    pos = pos0 + t * tt + lax.broadcasted_iota(jnp.int32, (tt, 1), 0)
    for gi, w in enumerate(POOL_WINDOWS):
        cols = slice(gi * G_A, (gi + 1) * G_A)
        win = u[:, cols]
        for j in range(1, w):
            win = win + ext_ref[POOL_HALO - j:POOL_HALO - j + tt, cols]
        cnt = jnp.minimum(pos + 1, w).astype(F32)
        pooled = win / cnt - u[:, cols]
        y = _dot(pooled.astype(BF16), w_ref[gi]) * scale_ref[:, cols]
        y_ref[0, :, cols] = y.astype(y_ref.dtype)
    tail = ext_ref[tt:tt + POOL_HALO, :]
    nbuf_ref[0] = tail
    ext_ref[0:POOL_HALO, :] = tail


def _pool_mix(u, buf16, w_pool, scale, *, pos0, tt):
    B, T, _ = u.shape
    tt = min(tt, T)
    assert T % tt == 0
    return pl.pallas_call(
        functools.partial(_pool_kernel, tt=tt, pos0=pos0),
        grid=(B, T // tt),
        in_specs=[pl.BlockSpec((1, tt, D_A), lambda b, t: (b, t, 0)),
                  pl.BlockSpec((1, POOL_HALO, D_A), lambda b, t: (b, 0, 0)),
                  pl.BlockSpec((len(POOL_WINDOWS), G_A, G_A), lambda b, t: (0, 0, 0)),
                  pl.BlockSpec((1, D_A), lambda b, t: (0, 0))],
        out_specs=[pl.BlockSpec((1, tt, D_A), lambda b, t: (b, t, 0)),
                   pl.BlockSpec((1, POOL_HALO, D_A), lambda b, t: (b, 0, 0))],
        out_shape=[jax.ShapeDtypeStruct((B, T, D_A), BF16),
                   jax.ShapeDtypeStruct((B, POOL_HALO, D_A), F32)],
        scratch_shapes=[pltpu.VMEM((POOL_HALO + tt, D_A), F32)],
        compiler_params=_cparams(("parallel", "arbitrary")),
        name="pool_mix",
    )(u, buf16, w_pool, scale)


def _hgrn_kernel(q_ref, f_ref, i_ref, g_ref, lbraw_ref, s0_ref, ng_ref, y_ref, sfin_ref,
                 st_ref, b_ref, kk_ref, v_ref, qd_ref, o_ref, *, layer, tt, t_valid):
    t = pl.program_id(1)
    nt = pl.num_programs(1)
    C = HGRN_CHUNK

    @pl.when(t == 0)
    def _():
        for h in range(H_B):
            st_ref[h] = s0_ref[0, h].T

    raw = lbraw_ref[...]
    e = jnp.exp(raw - jnp.max(raw, axis=0, keepdims=True))
    sm = e / jnp.sum(e, axis=0, keepdims=True)
    lb = jnp.zeros((1, D_B), F32)
    for j in range(1, layer + 1):
        lb = lb + sm[j:j + 1]
    lb = (lb + sm[0:1]) - sm[0:1]

    z = f_ref[0]
    la = jnp.log(lb)
    c = jnp.log1p(-lb) + _log_sigmoid(z)
    amax = jnp.maximum(la, c)
    delta = la - c
    lf = jnp.where(jnp.isnan(delta), la + c, amax + jnp.log1p(jnp.exp(-jnp.abs(delta))))
    kk = (1.0 - lb) * _sigmoid(-z)
    if t_valid is not None:
        row = t * tt + lax.broadcasted_iota(jnp.int32, (tt, 1), 0)
        lf = jnp.where(row < t_valid, lf, 0.0)
        kk = jnp.where(row < t_valid, kk, 0.0)
    r = lax.broadcasted_iota(jnp.int32, (tt, tt), 0)
    s = lax.broadcasted_iota(jnp.int32, (tt, tt), 1)
    tri = jnp.where((s <= r) & (s >= jnp.bitwise_and(r, -C)), 1.0, 0.0).astype(BF16)
    b_all = _dot01(tri, lf)
    qs_all = q_ref[0] * DK_B ** -0.5
    v_all = i_ref[0]
    heads = [slice(h * DK_B, (h + 1) * DK_B) for h in range(H_B)]
    halo = jnp.zeros((C, DK_B), F32)
    for h, hl in enumerate(heads):
        for ref, val in ((b_ref, b_all), (kk_ref, kk), (v_ref, v_all)):
            ref[h, 0:C, :] = halo
            ref[h, C:C + tt, :] = val[:, hl]

    tmod = jnp.bitwise_and(lax.broadcasted_iota(jnp.int32, (tt, 1), 0), C - 1)
    for h, hl in enumerate(heads):
        bh, qh = b_all[:, hl], qs_all[:, hl]
        acc = jnp.zeros((tt, DK_B), F32)
        for d in range(C):
            lag = slice(C - d, C - d + tt)
            w = qh * jnp.exp(jnp.minimum(bh - b_ref[h, lag, :], 0.0)) * kk_ref[h, lag, :]
            rs = jnp.sum(w, axis=-1, keepdims=True)
            acc = acc + jnp.where(tmod >= d, rs, 0.0) * v_ref[h, lag, :]
        o_ref[:, hl] = acc
        qd_ref[:, hl] = qh * jnp.exp(bh)

    def chunk(ci, carry):
        rows = pl.ds(pl.multiple_of(ci * C, C), C)
        rows_h = pl.ds(pl.multiple_of(ci * C + C, C), C)
        for h, hl in enumerate(heads):
            b = b_ref[h, rows_h, :]
            v = v_ref[h, rows_h, :]
            st = st_ref[h]
            o_ref[rows, hl] += _dot_nt(qd_ref[rows, hl].astype(BF16), st.astype(BF16))
            b_last = b[C - 1:C]
            k_dec = kk_ref[h, rows_h, :] * jnp.exp(b_last - b)
            st_ref[h] = st * jnp.exp(b_last) + _dot_tn(v.astype(BF16), k_dec.astype(BF16))
        return carry

    n_chunks = tt // C
    lax.fori_loop(0, n_chunks, chunk, 0, unroll=math.gcd(n_chunks, 8))

    for h, hl in enumerate(heads):
        o = o_ref[:, hl]
        o = o * lax.rsqrt(jnp.mean(jnp.square(o), axis=-1, keepdims=True) + RMS_EPS) * ng_ref[...]
        o = o * _sigmoid(g_ref[0, :, hl])
        y_ref[0, :, hl] = o.astype(y_ref.dtype)

    @pl.when(t == nt - 1)
    def _():
        for h in range(H_B):
            sfin_ref[0, h] = st_ref[h].T


def _hgrn2(zb, lb_raw, s0, norm_g, *, layer, tt, t_valid=None):
    B, T, _ = zb.shape
    tt = min(tt, T)
    assert T % tt == 0 and tt % HGRN_CHUNK == 0

    def col(j):
        return pl.BlockSpec((1, tt, D_B), lambda b, t: (b, t, j))

    return pl.pallas_call(
        functools.partial(_hgrn_kernel, layer=layer, tt=tt, t_valid=t_valid),
        grid=(B, T // tt),
        in_specs=[col(0), col(1), col(2), col(3),
                  pl.BlockSpec((DEPTH, D_B), lambda b, t: (0, 0)),
                  pl.BlockSpec((1, H_B, DK_B, DK_B), lambda b, t: (b, 0, 0, 0)),
                  pl.BlockSpec((1, DK_B), lambda b, t: (0, 0))],
        out_specs=[pl.BlockSpec((1, tt, D_B), lambda b, t: (b, t, 0)),
                   pl.BlockSpec((1, H_B, DK_B, DK_B), lambda b, t: (b, 0, 0, 0))],
        out_shape=[jax.ShapeDtypeStruct((B, T, D_B), BF16),
                   jax.ShapeDtypeStruct((B, H_B, DK_B, DK_B), F32)],
        scratch_shapes=[pltpu.VMEM((H_B, DK_B, DK_B), F32)]
        + [pltpu.VMEM((H_B, HGRN_CHUNK + tt, DK_B), F32)] * 3
        + [pltpu.VMEM((tt, D_B), F32)] * 2,
        compiler_params=_cparams(("parallel", "arbitrary")),
        name="hgrn2",
    )(zb, zb, zb, zb, lb_raw, s0, norm_g)


def _fox_gate_kernel(f_ref, fb_ref, logf_ref, c_ref, *, t_valid):
    S = f_ref.shape[1]
    logf = _log_sigmoid(f_ref[0] + fb_ref[...])
    logf_ref[0] = logf
    if t_valid is not None:
        row = lax.broadcasted_iota(jnp.int32, (S, 1), 0)
        logf = jnp.where(row < t_valid, logf, 0.0)
    r = lax.broadcasted_iota(jnp.int32, (S, S), 0)
    s = lax.broadcasted_iota(jnp.int32, (S, S), 1)
    tri = jnp.where(s <= r, 1.0, 0.0).astype(BF16)
    c_ref[0] = _dot01(tri, logf)


def _fox_gate(f_pad, fb_pad, *, t_valid=None):
    B, S, L = f_pad.shape
    spec = pl.BlockSpec((1, S, L), lambda b: (b, 0, 0))
    return pl.pallas_call(
        functools.partial(_fox_gate_kernel, t_valid=t_valid),
        grid=(B,),
        in_specs=[spec, pl.BlockSpec((1, L), lambda b: (0, 0))],
        out_specs=[spec, spec],
        out_shape=[jax.ShapeDtypeStruct((B, S, L), F32)] * 2,
        compiler_params=_cparams(("parallel",)),
        name="fox_gate",
    )(f_pad, fb_pad)


def _split3(c):
    hi = c.astype(BF16).astype(F32)
    r1 = c - hi
    mid = r1.astype(BF16).astype(F32)
    lo = (r1 - mid).astype(BF16).astype(F32)
    return hi, mid, lo


def _bias_lanes(c, sign, first):
    hi, mid, lo = _split3(c)
    lane = lax.broadcasted_iota(jnp.int32, (c.shape[0], LANES), 1)
    base = 0 if first else 3
    vals = jnp.where(lane == base, hi, jnp.where(lane == base + 1, mid, jnp.where(lane == base + 2, lo, 0.0))) * sign
    ones = jnp.where((lane >= 3 - base) & (lane < 6 - base), 1.0, 0.0)
    return (vals + ones).astype(BF16)


def _fox_prompt_kernel(q_ref, k_ref, v_ref, c_ref, o_ref, kaug_ref, vb_ref, *, tq):
    h = pl.program_id(1)
    qi = pl.program_id(2)
    S = k_ref.shape[1]
    scale = DH_C ** -0.5
    lane = lax.broadcasted_iota(jnp.int32, (tq, LANES), 1)

    def head_col(rows):
        return jnp.sum(jnp.where(lane == h, c_ref[0, rows, :], 0.0), axis=-1, keepdims=True)

    @pl.when(qi == 0)
    def _():
        for r in range(0, S, tq):
            rows = slice(r, r + tq)
            kaug_ref[rows, 0:DH_C] = (k_ref[0, rows, :] * scale).astype(BF16)
            kaug_ref[rows, DH_C:] = _bias_lanes(head_col(rows), -1.0, first=False)
            vb_ref[rows, :] = v_ref[0, rows, :].astype(BF16)

    q_rows = pl.ds(pl.multiple_of(qi * tq, tq), tq)
    q = jnp.concatenate([q_ref[0], _bias_lanes(head_col(q_rows), 1.0, first=True)], axis=1)

    def block(j, carry, masked):
        m, l, acc = carry
        ks = slice(j * tq, (j + 1) * tq)
        s = _dot_nt(q, kaug_ref[ks, :])
        if masked:
            rr = lax.broadcasted_iota(jnp.int32, (tq, tq), 0)
            cc = lax.broadcasted_iota(jnp.int32, (tq, tq), 1)
            s = jnp.where(cc <= rr, s, NEG_INF)
        m_new = jnp.maximum(m, jnp.max(s, axis=-1, keepdims=True))
        alpha = jnp.exp(m - m_new)
        p = jnp.exp(s - m_new)
        l = alpha * l + jnp.sum(p, axis=-1, keepdims=True)
        acc = alpha * acc + _dot(p.astype(BF16), vb_ref[ks, :])
        return m_new, l, acc

    init = (jnp.full((tq, 1), NEG_INF, F32), jnp.zeros((tq, 1), F32), jnp.zeros((tq, DH_C), F32))
    for n in range(S // tq):

        @pl.when(qi == n)
        def _(n=n):
            carry = block(n, init, True)
            for j in range(n):
                carry = block(j, carry, False)
            m, l, acc = carry
            o_ref[0] = (acc / l).astype(o_ref.dtype)


def _fox_prompt(q, k, v, c, *, tq=512):
    B, S, _ = q.shape
    tq = min(tq, S)
    assert S % tq == 0
    kv_spec = pl.BlockSpec((1, S, DH_C), lambda b, h, i: (b, 0, h))
    return pl.pallas_call(
        functools.partial(_fox_prompt_kernel, tq=tq),
        grid=(B, H_C, S // tq),
        in_specs=[pl.BlockSpec((1, tq, DH_C), lambda b, h, i: (b, i, h)), kv_spec, kv_spec,
                  pl.BlockSpec((1, S, LANES), lambda b, h, i: (b, 0, 0))],
        out_specs=pl.BlockSpec((1, tq, DH_C), lambda b, h, i: (b, i, h)),
        out_shape=jax.ShapeDtypeStruct((B, S, D_C), BF16),
        scratch_shapes=[pltpu.VMEM((S, 2 * DH_C), BF16), pltpu.VMEM((S, DH_C), BF16)],
        compiler_params=_cparams(("parallel", "parallel", "arbitrary")),
        name="fox_prompt",
    )(q, k, v, c)


def _fox_suffix_kernel(pt_ref, lf_ref, usel_ref, tsel_ref, esel_ref, o_ref, g_ref, *, n_pages):
    b = pl.program_id(0)
    for p in range(n_pages):
        g_ref[p:p + 1, :] = lf_ref[pl.ds(pt_ref[b, p], 1), :]
    g = g_ref[...]
    within = _dot01_rhs(g, usel_ref[...])
    tot = _dot01_rhs(g, tsel_ref[...])
    r = lax.broadcasted_iota(jnp.int32, (n_pages, n_pages), 0)
    s = lax.broadcasted_iota(jnp.int32, (n_pages, n_pages), 1)
    later = jnp.where(s > r, 1.0, 0.0).astype(BF16)
    carry = _dot01(later, tot)
    suf = within + _dot01_rhs(carry, esel_ref[...])
    for h in range(H_C):
        o_ref[0, h] = suf[:, h * PAGE_SIZE:(h + 1) * PAGE_SIZE]


def _dot01_rhs(x, sel):
    hi = x.astype(BF16)
    r1 = x - hi.astype(F32)
    mid = r1.astype(BF16)
    lo = (r1 - mid.astype(F32)).astype(BF16)
    return _dot(hi, sel) + _dot(mid, sel) + _dot(lo, sel)


def _fox_suffix(page_table, cache_logf_l):
    B, n_pages = page_table.shape
    n_pool, W = cache_logf_l.shape
    key_in = jnp.arange(W) // H_C
    head_in = jnp.arange(W) % H_C
    head_out = jnp.arange(W) // PAGE_SIZE
    key_out = jnp.arange(W) % PAGE_SIZE
    usel = ((head_in[:, None] == head_out[None, :]) & (key_in[:, None] > key_out[None, :])).astype(BF16)
    tsel = (head_in[:, None] == jnp.arange(LANES)[None, :]).astype(BF16)
    esel = (jnp.arange(LANES)[:, None] == head_out[None, :]).astype(BF16)
    full = lambda shp: pl.BlockSpec(shp, lambda b, pt: tuple(0 for _ in shp))
    grid_spec = pltpu.PrefetchScalarGridSpec(
        num_scalar_prefetch=1,
        grid=(B,),
        in_specs=[full((n_pool, W)), full((W, W)), full((W, LANES)), full((LANES, W))],
        out_specs=pl.BlockSpec((1, H_C, n_pages, PAGE_SIZE), lambda b, pt: (b, 0, 0, 0)),
        scratch_shapes=[pltpu.VMEM((n_pages, W), F32)],
    )
    return pl.pallas_call(
        functools.partial(_fox_suffix_kernel, n_pages=n_pages),
        grid_spec=grid_spec,
        out_shape=jax.ShapeDtypeStruct((B, H_C, n_pages, PAGE_SIZE), F32),
        compiler_params=_cparams(("arbitrary",)),
        name="fox_suffix",
    )(page_table, cache_logf_l, usel, tsel, esel)


def _fox_sample_kernel(pt_ref, q_ref, kn_ref, vn_ref, cq_ref, ckn_ref, suf_ref, *rest, t_new, ppg):
    k_refs = rest[:ppg]
    v_refs = rest[ppg:2 * ppg]
    o_ref, m_ref, l_ref, acc_ref = rest[2 * ppg:]
    g = pl.program_id(1)
    ng = pl.num_programs(1)
    scale = DH_C ** -0.5
    R = H_C * t_new

    def q_head(h):
        return q_ref[0, :, h * DH_C:(h + 1) * DH_C]

    def update(s, v_of_head):
        m = m_ref[...]
        m_new = jnp.maximum(m, jnp.max(s, axis=-1, keepdims=True))
        alpha = jnp.exp(m - m_new)
        p = jnp.exp(s - m_new)
        l_ref[...] = alpha * l_ref[...] + jnp.sum(p, axis=-1, keepdims=True)
        m_ref[...] = m_new
        for h in range(H_C):
            rows = slice(h * t_new, (h + 1) * t_new)
            acc_ref[rows, :] = alpha[rows] * acc_ref[rows, :] + _dot(p[rows].astype(BF16), v_of_head(h))

    @pl.when(g == 0)
    def _():
        m_ref[...] = jnp.full_like(m_ref, NEG_INF)
        l_ref[...] = jnp.zeros_like(l_ref)
        acc_ref[...] = jnp.zeros_like(acc_ref)
        s = jnp.concatenate(
            [_dot_nt(q_head(h), kn_ref[0, :, h * DH_C:(h + 1) * DH_C]) for h in range(H_C)], axis=0) * scale
        s = s + (cq_ref[0] - ckn_ref[0])
        qpos = lax.broadcasted_iota(jnp.int32, s.shape, 0) % t_new
        kpos = lax.broadcasted_iota(jnp.int32, s.shape, 1)
        s = jnp.where(kpos <= qpos, s, NEG_INF)
        update(s, lambda h: vn_ref[0, :, h * DH_C:(h + 1) * DH_C])

    def head_rows(ref, h):
        return ref[0, 0, pl.ds(h, PAGE_SIZE, stride=H_C), :].astype(BF16)

    for j in range(ppg):
        kj, vj = k_refs[j], v_refs[j]
        s = jnp.concatenate([_dot_nt(q_head(h), head_rows(kj, h)) for h in range(H_C)], axis=0) * scale
        bias = jnp.concatenate(
            [jnp.broadcast_to(suf_ref[0, h, pl.ds(g * ppg + j, 1), :], (t_new, PAGE_SIZE)) for h in range(H_C)], axis=0)
        s = s + (cq_ref[0] + bias)
        update(s, lambda h: head_rows(vj, h))

    @pl.when(g == ng - 1)
    def _():
        o = acc_ref[...] / l_ref[...]
        for h in range(H_C):
            o_ref[0, :, h * DH_C:(h + 1) * DH_C] = o[h * t_new:(h + 1) * t_new].astype(o_ref.dtype)


def _fox_sample(q, k_new, v_new, cq_col, ck_new, suffix, cache_k, cache_v, page_table, *, layer, t_new, ppg=8):
    B, n_pages = page_table.shape
    T = q.shape[1]
    T16 = k_new.shape[1]
    R = H_C * t_new
    assert n_pages % ppg == 0

    def page_spec(j):
        return pl.BlockSpec((1, 1, PAGE_SIZE * H_C, DH_C),
                            lambda b, g, pt: (layer, pt[b, g * ppg + j], 0, 0))

    bspec = lambda shp: pl.BlockSpec((1,) + shp, lambda b, g, pt: (b,) + tuple(0 for _ in shp))
    grid_spec = pltpu.PrefetchScalarGridSpec(
        num_scalar_prefetch=1,
        grid=(B, n_pages // ppg),
        in_specs=[bspec((T, D_C)), bspec((T16, D_C)), bspec((T16, D_C)), bspec((R, 1)), bspec((R, T16)),
                  bspec((H_C, n_pages, PAGE_SIZE))]
        + [page_spec(j) for j in range(ppg)] + [page_spec(j) for j in range(ppg)],
        out_specs=bspec((T, D_C)),
        scratch_shapes=[pltpu.VMEM((R, 1), F32), pltpu.VMEM((R, 1), F32), pltpu.VMEM((R, DH_C), F32)],
    )
    return pl.pallas_call(
        functools.partial(_fox_sample_kernel, t_new=t_new, ppg=ppg),
        grid_spec=grid_spec,
        out_shape=jax.ShapeDtypeStruct((B, T, D_C), BF16),
        compiler_params=_cparams(("parallel", "arbitrary")),
        name="fox_sample",
    )(page_table, q, k_new, v_new, cq_col, ck_new, suffix, *([cache_k] * ppg), *([cache_v] * ppg))


def _merge_kernel(ya_ref, yb_ref, yc_ref, wa_ref, wb_ref, wc_ref, ga_ref, gb_ref, gc_ref, o_ref):
    m = ga_ref[...].astype(F32) * _dot(ya_ref[...], wa_ref[...])
    m = m + gb_ref[...].astype(F32) * _dot(yb_ref[...], wb_ref[...])
    m = m + gc_ref[...].astype(F32) * _dot(yc_ref[...], wc_ref[...])
    o_ref[...] = m.astype(o_ref.dtype)


def _merge(y_a, y_b, y_c, w_a, w_b, w_c, gates, *, tm=1024, tn=512):
    M = y_a.shape[0]
    tm = min(tm, M)
    nj = D_MODEL // tn
    act = lambda K: pl.BlockSpec((tm, K), lambda i, j: (i, 0))
    wsp = lambda K: pl.BlockSpec((K, tn), lambda i, j: (0, j))
    gsp = lambda o: pl.BlockSpec((tm, tn), lambda i, j: (i, j + o * nj))
    return pl.pallas_call(
        _merge_kernel,
        grid=(M // tm, nj),
        in_specs=[act(D_A), act(D_B), act(D_C), wsp(D_A), wsp(D_B), wsp(D_C), gsp(0), gsp(1), gsp(2)],
        out_specs=pl.BlockSpec((tm, tn), lambda i, j: (i, j)),
        out_shape=jax.ShapeDtypeStruct((M, D_MODEL), BF16),
        compiler_params=_cparams(("parallel", "parallel")),
        name="merge",
    )(y_a, y_b, y_c, w_a, w_b, w_c, gates, gates, gates)


def _ln_kernel(*refs, n_delta):
    x_ref = refs[0]
    d_refs = refs[1:1 + n_delta]
    g_ref, b_ref, o_ref, ob_ref = refs[1 + n_delta:]
    y = DN_ALPHA * x_ref[...]
    for d in d_refs:
        y = y + d[...].astype(F32)
    mu = jnp.mean(y, axis=-1, keepdims=True)
    yc = y - mu
    var = jnp.mean(jnp.square(yc), axis=-1, keepdims=True)
    o = yc * lax.rsqrt(var + LN_EPS) * g_ref[...] + b_ref[...]
    o_ref[...] = o
    ob_ref[...] = o.astype(BF16)


def _res_ln(x, deltas, g, b, *, tm=128):
    M = x.shape[0]
    tm = min(tm, M)
    row = pl.BlockSpec((tm, D_MODEL), lambda i: (i, 0))
    vec = pl.BlockSpec((1, D_MODEL), lambda i: (0, 0))
    return pl.pallas_call(
        functools.partial(_ln_kernel, n_delta=len(deltas)),
        grid=(M // tm,),
        in_specs=[row] * (1 + len(deltas)) + [vec, vec],
        out_specs=[row, row],
        out_shape=[jax.ShapeDtypeStruct((M, D_MODEL), F32), jax.ShapeDtypeStruct((M, D_MODEL), BF16)],
        compiler_params=_cparams(("parallel",)),
        name="res_ln",
    )(x, *deltas, g, b)


OFF_A = 0
OFF_B = D_A
OFF_QC = OFF_B + 4 * D_B
OFF_KC = OFF_QC + D_C
OFF_VC = OFF_KC + D_C
OFF_FC = OFF_VC + D_C
N_GATES = 3 * D_MODEL


def _in_proj(xpb, xsb, w_in_t, layer):
    seg = lambda **kw: _proj(xpb, xsb, w_in_t, layer, transposed=True, **kw)
    parts = {
        "u_a": seg(n_off=OFF_A, n_out=D_A),
        "zb": seg(n_off=OFF_B, n_out=4 * D_B),
        "q_c": seg(n_off=OFF_QC, n_out=D_C, out_dtype=BF16),
        "k_c": seg(n_off=OFF_KC, n_out=D_C),
        "v_c": seg(n_off=OFF_VC, n_out=D_C),
        "f_c": seg(n_off=OFF_FC, n_out=LANES, tn=LANES),
        "gates": seg(n_off=OFF_FC + H_C, n_out=N_GATES, out_dtype=BF16, act="sigmoid"),
    }
    return ({k: v[0] for k, v in parts.items()}, {k: v[1] for k, v in parts.items()})


def _mixers(z, lw, *, layer, B, T, pool_buf16, pos0, hgrn_s0, lb_raw, sample=None):
    M = B * T
    tm = min(PROJ_TM, M)
    u_a, zb, q_c, k_c, v_c, f_c, gates = (z[k] for k in ("u_a", "zb", "q_c", "k_c", "v_c", "f_c", "gates"))

    y_a, nbuf = _pool_mix(u_a.reshape(B, T, D_A), pool_buf16, lw["w_pool"], lw["pool_scale"], pos0=pos0, tt=256)
    new_buf = nbuf[:, 1:]

    zb3 = zb.reshape(B, T, 4 * D_B)
    if T % HGRN_CHUNK:
        Tp = -(-T // HGRN_CHUNK) * HGRN_CHUNK
        zb3 = jnp.pad(zb3, ((0, 0), (0, Tp - T), (0, 0)))
        y_b, new_s = _hgrn2(zb3, lb_raw, hgrn_s0, lw["hgrn_norm_g"], layer=layer, tt=128, t_valid=T)
        y_b = y_b[:, :T]
    else:
        y_b, new_s = _hgrn2(zb3, lb_raw, hgrn_s0, lw["hgrn_norm_g"], layer=layer, tt=128)

    q3, k3, v3 = (a.reshape(B, T, D_C) for a in (q_c, k_c, v_c))
    if sample is None:
        logf, c = _fox_gate(f_c.reshape(B, T, LANES), lw["fox_fb"])
        y_c = _fox_prompt(q3, k3, v3, c)
    else:
        T16 = PAGE_SIZE
        f3 = jnp.pad(f_c.reshape(B, T, LANES), ((0, 0), (0, T16 - T), (0, 0)))
        logf, c = _fox_gate(f3, lw["fox_fb"], t_valid=T)
        logf = logf[:, :T]
        c_hm = jnp.transpose(c[:, :, :H_C], (0, 2, 1))
        cq_col = c_hm[:, :, :T].reshape(B, H_C * T, 1)
        ck_new = jnp.repeat(c_hm, T, axis=1)
        pad_kv = lambda a: jnp.pad(a.astype(BF16), ((0, 0), (0, T16 - T), (0, 0)))
        suffix = _fox_suffix(sample["page_table"], sample["cache_logf"][layer])
        y_c = _fox_sample(q3, pad_kv(k3), pad_kv(v3), cq_col, ck_new, suffix, sample["cache_k"], sample["cache_v"],
                          sample["page_table"], layer=layer, t_new=T)
    logf = logf[:, :, :H_C]

    merged = _merge(y_a.reshape(M, D_A), y_b.reshape(M, D_B), y_c.reshape(M, D_C),
                    lw["w_up_a"], lw["w_up_b"], lw["w_up_c"], gates, tm=tm)
    return merged, new_buf, new_s, logf


def kernel(x_prompt, x_sample, cache_k, cache_v, cache_logf, state_hgrn, state_pool, page_table, p_prompt, p_sample,
           w_in, fox_fb, w_pool, pool_scale, hgrn_lb, hgrn_norm_g, w_up_a, w_up_b, w_up_c, w_out, ln1_g, ln1_b,
           w_ff_up, w_ff_down, w_ple, w_ple_gate, ln2_g, ln2_b):
    Bp, Tp, _ = x_prompt.shape
    Bs, Ts, _ = x_sample.shape
    n_pool = cache_k.shape[1]
    past_len = page_table.shape[1] * PAGE_SIZE
    cache_logf2 = cache_logf.reshape(DEPTH, n_pool, PAGE_SIZE * H_C)
    page_rows = (DEPTH, n_pool, PAGE_SIZE * H_C, DH_C)
    sample_ctx = {"page_table": page_table, "cache_k": cache_k.reshape(page_rows),
                  "cache_v": cache_v.reshape(page_rows), "cache_logf": cache_logf2}

    xp = x_prompt.reshape(Bp * Tp, D_MODEL)
    xs = x_sample.reshape(Bs * Ts, D_MODEL)
    xpb, xsb = xp.astype(BF16), xs.astype(BF16)
    zero_buf = jnp.zeros((Bp, POOL_HALO, D_A), F32)
    zero_state = jnp.zeros((Bp, H_B, DK_B, DK_B), F32)
    lb_raw = hgrn_lb.astype(F32)
    w_in_t = jnp.swapaxes(w_in, 1, 2)
    w_down = w_ff_down.astype(BF16)

    outs = [[] for _ in range(10)]
    for i in range(DEPTH):
        lw = {
            "fox_fb": jnp.pad(fox_fb[i].astype(F32), (0, LANES - H_C)).reshape(1, LANES),
            "w_pool": w_pool[i].astype(BF16),
            "pool_scale": pool_scale[i].astype(F32).reshape(1, D_A),
            "hgrn_norm_g": hgrn_norm_g[i].astype(F32).reshape(1, DK_B),
            "w_up_a": w_up_a[i].astype(BF16), "w_up_b": w_up_b[i].astype(BF16), "w_up_c": w_up_c[i].astype(BF16),
        }
        vec = lambda a: a[i].astype(F32).reshape(1, D_MODEL)
        pp = p_prompt[i].reshape(Bp * Tp, -1).astype(BF16)
        ps = p_sample[i].reshape(Bs * Ts, -1).astype(BF16)

        zp, zs = _in_proj(xpb, xsb, w_in_t, i)
        merged_p, buf_p, s_p, f_p = _mixers(
            zp, lw, layer=i, B=Bp, T=Tp, pool_buf16=zero_buf, pos0=0, hgrn_s0=zero_state, lb_raw=lb_raw)
        buf16 = jnp.pad(state_pool[i], ((0, 0), (POOL_HALO - POOL_BUF, 0), (0, 0)))
        merged_s, buf_s, s_s, f_s = _mixers(
            zs, lw, layer=i, B=Bs, T=Ts, pool_buf16=buf16, pos0=past_len, hgrn_s0=state_hgrn[i], lb_raw=lb_raw,
            sample=sample_ctx)

        mix_p, mix_s = _proj(merged_p, merged_s, w_out, i)
        hp, hpb = _res_ln(xp, [mix_p], vec(ln1_g), vec(ln1_b))
        hs, hsb = _res_ln(xs, [mix_s], vec(ln1_g), vec(ln1_b))
        up_p, up_s = _proj(hpb, hsb, w_ff_up, i, out_dtype=BF16, act="relu2")
        ff_p = _mm(up_p, w_down, i)
        ff_s = _mm(up_s, w_down, i)
        ple_p, ple_s = _proj(hpb, hsb, w_ple_gate, i, act="sigmoid", side=(pp, ps, w_ple), tm=PROJ_TM // 2)
        xp, xpb = _res_ln(hp, [ff_p, ple_p], vec(ln2_g), vec(ln2_b))
        xs, xsb = _res_ln(hs, [ff_s, ple_s], vec(ln2_g), vec(ln2_b))

        vals = (zp["k_c"].reshape(Bp, Tp, H_C, DH_C), zp["v_c"].reshape(Bp, Tp, H_C, DH_C), f_p,
                zs["k_c"].reshape(Bs, Ts, H_C, DH_C), zs["v_c"].reshape(Bs, Ts, H_C, DH_C), f_s,
                s_p, s_s, buf_p, buf_s)
        for o, val in zip(outs, vals):
            o.append(val)
    return (xp.reshape(Bp, Tp, D_MODEL), xs.reshape(Bs, Ts, D_MODEL)) + tuple(jnp.stack(o) for o in outs)
```

```python
import functools
import math

import jax
import jax.numpy as jnp
from jax import lax
from jax.experimental import pallas as pl
from jax.experimental.pallas import tpu as pltpu

F32 = jnp.float32
BF16 = jnp.bfloat16

D_MODEL = 4096
DEPTH = 2
PAGE_SIZE = 128
POOL_WINDOWS = (2, 4, 8, 16)
D_A = D_MODEL // 4
G_A = D_A // len(POOL_WINDOWS)
POOL_BUF = 15
D_B = D_MODEL // 4
DK_B = 128
H_B = D_B // DK_B
D_C = D_MODEL // 2
DH_C = 128
H_C = D_C // DH_C
D_FF = 4 * D_MODEL
DN_ALPHA = (2 * DEPTH) ** 0.25
LN_EPS = 1e-5
RMS_EPS = 1e-6

LANES = 128
SUBLANES = 8
VMEM_LIMIT = 58 * 1024 * 1024

HGRN_CHUNK = 16
HGRN_ROWS = 32
POOL_HALO = 16

NEG_INF = float("-inf")


def _cparams(sem):
    return pltpu.CompilerParams(dimension_semantics=sem, vmem_limit_bytes=VMEM_LIMIT)


def _dot(a, b):
    return jnp.dot(a, b, preferred_element_type=F32)


def _dot_nt(a, b):
    return lax.dot_general(a, b, (((1,), (1,)), ((), ())), preferred_element_type=F32)


def _dot_tn(a, b):
    return lax.dot_general(a, b, (((0,), (0,)), ((), ())), preferred_element_type=F32)


def _dot01(sel, x):
    hi = x.astype(BF16)
    r1 = x - hi.astype(F32)
    mid = r1.astype(BF16)
    lo = (r1 - mid.astype(F32)).astype(BF16)
    return _dot(sel, hi) + _dot(sel, mid) + _dot(sel, lo)


def _log_sigmoid(z):
    return jnp.minimum(z, 0.0) - jnp.log1p(jnp.exp(-jnp.abs(z)))


def _sigmoid(z):
    return 1.0 / (1.0 + jnp.exp(-z))


def _mm_kernel(x_ref, w_ref, o_ref, acc_ref):
    k = pl.program_id(2)

    @pl.when(k == 0)
    def _():
        acc_ref[...] = jnp.zeros_like(acc_ref)

    acc_ref[...] += _dot(x_ref[...], w_ref[...])

    @pl.when(k == pl.num_programs(2) - 1)
    def _():
        o_ref[...] = acc_ref[...].astype(o_ref.dtype)


def _mm(x, w, layer, *, out_dtype=BF16, tm=1024, tn=1024, tk=4096):
    M, K = x.shape
    N = w.shape[2]
    tm, tn, tk = min(tm, M), min(tn, N), min(tk, K)
    assert M % tm == 0 and N % tn == 0 and K % tk == 0
    return pl.pallas_call(
        _mm_kernel,
        grid=(M // tm, N // tn, K // tk),
        in_specs=[pl.BlockSpec((tm, tk), lambda i, j, k: (i, k)),
                  pl.BlockSpec((None, tk, tn), lambda i, j, k: (layer, k, j))],
        out_specs=pl.BlockSpec((tm, tn), lambda i, j, k: (i, j)),
        out_shape=jax.ShapeDtypeStruct((M, N), out_dtype),
        scratch_shapes=[pltpu.VMEM((tm, tn), F32)],
        compiler_params=_cparams(("parallel", "parallel", "arbitrary")),
        name="dense",
    )(x, w)


PROJ_TM = 1024
PROJ_TN = 1024
CAST_ROWS = 256


def _proj_kernel(*refs, act, has_side, transposed, layer, start, tn):
    refs = list(refs)
    xp_ref, xs_ref, w_hbm = refs[:3]
    del refs[:3]
    if has_side:
        pp_ref, ps_ref, w2_ref = refs[:3]
        del refs[:3]
    op_ref, os_ref, wf32_ref, wbf_ref, sem = refs[:5]
    w2bf_ref = refs[5] if has_side else None
    j, i = pl.program_id(0), pl.program_id(1)
    nj = pl.num_programs(0)
    n_rows = wbf_ref.shape[0]
    mm = _dot_nt if transposed else _dot

    def tile_copy(jj):
        cols = pl.ds(pl.multiple_of(start + jj * tn, math.gcd(start, tn)), tn)
        src = w_hbm.at[layer, cols, :] if transposed else w_hbm.at[layer, :, cols]
        return pltpu.make_async_copy(src, wf32_ref, sem)

    def finish(acc, p_ref, o_ref):
        if act == "sigmoid":
            acc = _sigmoid(acc)
        elif act == "relu2":
            acc = jnp.square(jnp.maximum(acc, 0.0))
        if has_side:
            acc = _dot(p_ref[...], w2bf_ref[...]) * acc
        o_ref[...] = acc.astype(o_ref.dtype)

    @pl.when(i == 0)
    def _():
        @pl.when(j == 0)
        def _():
            tile_copy(0).start()

        tile_copy(j).wait()
        for r in range(0, n_rows, CAST_ROWS):
            n = min(CAST_ROWS, n_rows - r)
            wbf_ref[r:r + n, :] = wf32_ref[r:r + n, :].astype(BF16)
        if has_side:
            w2bf_ref[...] = w2_ref[...].astype(BF16)
        finish(mm(xs_ref[...], wbf_ref[...]), ps_ref if has_side else None, os_ref)

    @pl.when((i == 1) & (j + 1 < nj))
    def _():
        tile_copy(j + 1).start()

    @pl.when(i > 0)
    def _():
        finish(mm(xp_ref[...], wbf_ref[...]), pp_ref if has_side else None, op_ref)


def _proj(xp, xs, w, layer, *, n_off=0, n_out=None, out_dtype=F32, act=None, side=None, transposed=False,
          tm=PROJ_TM, tn=PROJ_TN):
    Mp, K = xp.shape
    Ms = xs.shape[0]
    n_all = w.shape[1] if transposed else w.shape[2]
    n_out = n_all - n_off if n_out is None else n_out
    tm = min(tm, Mp)
    assert Mp % tm == 0 and n_out % tn == 0
    assert n_off % (SUBLANES if transposed else LANES) == 0
    nmp = Mp // tm
    row = lambda i: jnp.maximum(i - 1, 0)
    in_specs = [pl.BlockSpec((tm, K), lambda j, i: (row(i), 0)), pl.BlockSpec((Ms, K), lambda j, i: (0, 0)),
                pl.BlockSpec(memory_space=pl.ANY)]
    args = [xp, xs, w]
    w_tile = (tn, K) if transposed else (K, tn)
    scratch = [pltpu.VMEM(w_tile, F32), pltpu.VMEM(w_tile, BF16), pltpu.SemaphoreType.DMA]
    if side is not None:
        pp, ps, w2 = side
        K2 = pp.shape[1]
        in_specs += [pl.BlockSpec((tm, K2), lambda j, i: (row(i), 0)),
                     pl.BlockSpec((Ms, K2), lambda j, i: (0, 0)),
                     pl.BlockSpec((None, K2, tn), lambda j, i: (layer, 0, j))]
        args += [pp, ps, w2]
        scratch.append(pltpu.VMEM((K2, tn), BF16))
    return pl.pallas_call(
        functools.partial(_proj_kernel, act=act, has_side=side is not None, transposed=transposed, layer=layer,
                          start=n_off, tn=tn),
        grid=(n_out // tn, nmp + 1),
        in_specs=in_specs,
        out_specs=[pl.BlockSpec((tm, tn), lambda j, i: (row(i), j)),
                   pl.BlockSpec((Ms, tn), lambda j, i: (0, j))],
        out_shape=[jax.ShapeDtypeStruct((Mp, n_out), out_dtype), jax.ShapeDtypeStruct((Ms, n_out), out_dtype)],
        scratch_shapes=scratch,
        compiler_params=_cparams(("arbitrary", "arbitrary")),
        name="proj",
    )(*args)


def _pool_kernel(u_ref, buf_ref, w_ref, scale_ref, y_ref, nbuf_ref, ext_ref, *, tt, pos0):
    t = pl.program_id(1)

    @pl.when(t == 0)
    def _():
        ext_ref[0:POOL_HALO, :] = buf_ref[0]

    u = u_ref[0]
    ext_ref[POOL_HALO:POOL_HALO + tt, :] = u
    pos = pos0 + t * tt + lax.broadcasted_iota(jnp.int32, (tt, 1), 0)
    for gi, w in enumerate(POOL_WINDOWS):
        cols = slice(gi * G_A, (gi + 1) * G_A)
        win = u[:, cols]
        for j in range(1, w):
            win = win + ext_ref[POOL_HALO - j:POOL_HALO - j + tt, cols]
        cnt = jnp.minimum(pos + 1, w).astype(F32)
        pooled = win / cnt - u[:, cols]
        y = _dot(pooled.astype(BF16), w_ref[gi]) * scale_ref[:, cols]
        y_ref[0, :, cols] = y.astype(y_ref.dtype)
    tail = ext_ref[tt:tt + POOL_HALO, :]
    nbuf_ref[0] = tail
    ext_ref[0:POOL_HALO, :] = tail


def _pool_mix(u, buf16, w_pool, scale, *, pos0, tt):
    B, T, _ = u.shape
    tt = min(tt, T)
    assert T % tt == 0
    return pl.pallas_call(
        functools.partial(_pool_kernel, tt=tt, pos0=pos0),
        grid=(B, T // tt),
        in_specs=[pl.BlockSpec((1, tt, D_A), lambda b, t: (b, t, 0)),
                  pl.BlockSpec((1, POOL_HALO, D_A), lambda b, t: (b, 0, 0)),
                  pl.BlockSpec((len(POOL_WINDOWS), G_A, G_A), lambda b, t: (0, 0, 0)),
                  pl.BlockSpec((1, D_A), lambda b, t: (0, 0))],
        out_specs=[pl.BlockSpec((1, tt, D_A), lambda b, t: (b, t, 0)),
                   pl.BlockSpec((1, POOL_HALO, D_A), lambda b, t: (b, 0, 0))],
        out_shape=[jax.ShapeDtypeStruct((B, T, D_A), BF16),
                   jax.ShapeDtypeStruct((B, POOL_HALO, D_A), F32)],
        scratch_shapes=[pltpu.VMEM((POOL_HALO + tt, D_A), F32)],
        compiler_params=_cparams(("parallel", "arbitrary")),
        name="pool_mix",
    )(u, buf16, w_pool, scale)


def _hgrn_kernel(q_ref, f_ref, i_ref, g_ref, lbraw_ref, s0_ref, ng_ref, y_ref, sfin_ref,
                 st_ref, b_ref, kk_ref, v_ref, qd_ref, o_ref, *, layer, tt, t_valid):
    t = pl.program_id(1)
    nt = pl.num_programs(1)
    C = HGRN_CHUNK

    @pl.when(t == 0)
    def _():
        for h in range(H_B):
            st_ref[h] = s0_ref[0, h].T

    raw = lbraw_ref[...]
    e = jnp.exp(raw - jnp.max(raw, axis=0, keepdims=True))
    sm = e / jnp.sum(e, axis=0, keepdims=True)
    lb = jnp.zeros((1, D_B), F32)
    for j in range(1, layer + 1):
        lb = lb + sm[j:j + 1]
    lb = (lb + sm[0:1]) - sm[0:1]

    z = f_ref[0]
    la = jnp.log(lb)
    c = jnp.log1p(-lb) + _log_sigmoid(z)
    amax = jnp.maximum(la, c)
    delta = la - c
    lf = jnp.where(jnp.isnan(delta), la + c, amax + jnp.log1p(jnp.exp(-jnp.abs(delta))))
    kk = (1.0 - lb) * _sigmoid(-z)
    if t_valid is not None:
        row = t * tt + lax.broadcasted_iota(jnp.int32, (tt, 1), 0)
        lf = jnp.where(row < t_valid, lf, 0.0)
        kk = jnp.where(row < t_valid, kk, 0.0)
    r = lax.broadcasted_iota(jnp.int32, (tt, tt), 0)
    s = lax.broadcasted_iota(jnp.int32, (tt, tt), 1)
    tri = jnp.where((s <= r) & (s >= jnp.bitwise_and(r, -C)), 1.0, 0.0).astype(BF16)
    b_all = _dot01(tri, lf)
    qs_all = q_ref[0] * DK_B ** -0.5
    v_all = i_ref[0]
    heads = [slice(h * DK_B, (h + 1) * DK_B) for h in range(H_B)]
    halo = jnp.zeros((C, DK_B), F32)
    for h, hl in enumerate(heads):
        for ref, val in ((b_ref, b_all), (kk_ref, kk), (v_ref, v_all)):
            ref[h, 0:C, :] = halo
            ref[h, C:C + tt, :] = val[:, hl]

    nr = min(tt, HGRN_ROWS)
    tmod = jnp.bitwise_and(lax.broadcasted_iota(jnp.int32, (nr, 1), 0), C - 1)
    for h, hl in enumerate(heads):
        for r0 in range(0, tt, nr):
            rows = slice(r0, r0 + nr)
            bh, qh = b_all[rows, hl], qs_all[rows, hl]
            cur = slice(C + r0, C + r0 + nr)
            acc = jnp.sum(qh * kk_ref[h, cur, :], axis=-1, keepdims=True) * v_ref[h, cur, :]
            for d in range(1, C):
                lag = slice(C + r0 - d, C + r0 - d + nr)
                w = qh * jnp.exp(jnp.minimum(bh - b_ref[h, lag, :], 0.0)) * kk_ref[h, lag, :]
                rs = jnp.sum(w, axis=-1, keepdims=True)
                acc = acc + jnp.where(tmod >= d, rs, 0.0) * v_ref[h, lag, :]
            o_ref[rows, hl] = acc
            qd_ref[rows, hl] = qh * jnp.exp(bh)

    def chunk(ci, carry):
        rows = pl.ds(pl.multiple_of(ci * C, C), C)
        rows_h = pl.ds(pl.multiple_of(ci * C + C, C), C)
        for h, hl in enumerate(heads):
            b = b_ref[h, rows_h, :]
            v = v_ref[h, rows_h, :]
            st = st_ref[h]
            o_ref[rows, hl] += _dot_nt(qd_ref[rows, hl].astype(BF16), st.astype(BF16))
            b_last = b[C - 1:C]
            k_dec = kk_ref[h, rows_h, :] * jnp.exp(b_last - b)
            st_ref[h] = st * jnp.exp(b_last) + _dot_tn(v.astype(BF16), k_dec.astype(BF16))
        return carry

    n_chunks = tt // C
    lax.fori_loop(0, n_chunks, chunk, 0, unroll=math.gcd(n_chunks, 8))

    for h, hl in enumerate(heads):
        o = o_ref[:, hl]
        o = o * lax.rsqrt(jnp.mean(jnp.square(o), axis=-1, keepdims=True) + RMS_EPS) * ng_ref[...]
        o = o * _sigmoid(g_ref[0, :, hl])
        y_ref[0, :, hl] = o.astype(y_ref.dtype)

    @pl.when(t == nt - 1)
    def _():
        for h in range(H_B):
            sfin_ref[0, h] = st_ref[h].T


def _hgrn2(zb, lb_raw, s0, norm_g, *, layer, tt, t_valid=None):
    B, T, _ = zb.shape
    tt = min(tt, T)
    assert T % tt == 0 and tt % HGRN_CHUNK == 0

    def col(j):
        return pl.BlockSpec((1, tt, D_B), lambda b, t: (b, t, j))

    return pl.pallas_call(
        functools.partial(_hgrn_kernel, layer=layer, tt=tt, t_valid=t_valid),
        grid=(B, T // tt),
        in_specs=[col(0), col(1), col(2), col(3),
                  pl.BlockSpec((DEPTH, D_B), lambda b, t: (0, 0)),
                  pl.BlockSpec((1, H_B, DK_B, DK_B), lambda b, t: (b, 0, 0, 0)),
                  pl.BlockSpec((1, DK_B), lambda b, t: (0, 0))],
        out_specs=[pl.BlockSpec((1, tt, D_B), lambda b, t: (b, t, 0)),
                   pl.BlockSpec((1, H_B, DK_B, DK_B), lambda b, t: (b, 0, 0, 0))],
        out_shape=[jax.ShapeDtypeStruct((B, T, D_B), BF16),
                   jax.ShapeDtypeStruct((B, H_B, DK_B, DK_B), F32)],
        scratch_shapes=[pltpu.VMEM((H_B, DK_B, DK_B), F32)]
        + [pltpu.VMEM((H_B, HGRN_CHUNK + tt, DK_B), F32)] * 3
        + [pltpu.VMEM((tt, D_B), F32)] * 2,
        compiler_params=_cparams(("parallel", "arbitrary")),
        name="hgrn2",
    )(zb, zb, zb, zb, lb_raw, s0, norm_g)


def _fox_gate_kernel(f_ref, fb_ref, logf_ref, c_ref, *, t_valid):
    S = f_ref.shape[1]
    logf = _log_sigmoid(f_ref[0] + fb_ref[...])
    logf_ref[0] = logf
    if t_valid is not None:
        row = lax.broadcasted_iota(jnp.int32, (S, 1), 0)
        logf = jnp.where(row < t_valid, logf, 0.0)
    r = lax.broadcasted_iota(jnp.int32, (S, S), 0)
    s = lax.broadcasted_iota(jnp.int32, (S, S), 1)
    tri = jnp.where(s <= r, 1.0, 0.0).astype(BF16)
    c_ref[0] = _dot01(tri, logf)


def _fox_gate(f_pad, fb_pad, *, t_valid=None):
    B, S, L = f_pad.shape
    spec = pl.BlockSpec((1, S, L), lambda b: (b, 0, 0))
    return pl.pallas_call(
        functools.partial(_fox_gate_kernel, t_valid=t_valid),
        grid=(B,),
        in_specs=[spec, pl.BlockSpec((1, L), lambda b: (0, 0))],
        out_specs=[spec, spec],
        out_shape=[jax.ShapeDtypeStruct((B, S, L), F32)] * 2,
        compiler_params=_cparams(("parallel",)),
        name="fox_gate",
    )(f_pad, fb_pad)


def _split3(c):
    hi = c.astype(BF16).astype(F32)
    r1 = c - hi
    mid = r1.astype(BF16).astype(F32)
    lo = (r1 - mid).astype(BF16).astype(F32)
    return hi, mid, lo


def _bias_lanes(c, sign, first):
    hi, mid, lo = _split3(c)
    lane = lax.broadcasted_iota(jnp.int32, (c.shape[0], LANES), 1)
    base = 0 if first else 3
    vals = jnp.where(lane == base, hi, jnp.where(lane == base + 1, mid, jnp.where(lane == base + 2, lo, 0.0))) * sign
    ones = jnp.where((lane >= 3 - base) & (lane < 6 - base), 1.0, 0.0)
    return (vals + ones).astype(BF16)


def _fox_prompt_kernel(q_ref, k_ref, v_ref, c_ref, o_ref, kaug_ref, vb_ref, *, tq):
    h = pl.program_id(1)
    qi = pl.program_id(2)
    S = k_ref.shape[1]
    scale = DH_C ** -0.5
    lane = lax.broadcasted_iota(jnp.int32, (tq, LANES), 1)

    def head_col(rows):
        return jnp.sum(jnp.where(lane == h, c_ref[0, rows, :], 0.0), axis=-1, keepdims=True)

    @pl.when(qi == 0)
    def _():
        for r in range(0, S, tq):
            rows = slice(r, r + tq)
            kaug_ref[rows, 0:DH_C] = (k_ref[0, rows, :] * scale).astype(BF16)
            kaug_ref[rows, DH_C:] = _bias_lanes(head_col(rows), -1.0, first=False)
            vb_ref[rows, :] = v_ref[0, rows, :].astype(BF16)

    q_rows = pl.ds(pl.multiple_of(qi * tq, tq), tq)
    q = jnp.concatenate([q_ref[0], _bias_lanes(head_col(q_rows), 1.0, first=True)], axis=1)

    def block(j, carry, masked):
        m, l, acc = carry
        ks = slice(j * tq, (j + 1) * tq)
        s = _dot_nt(q, kaug_ref[ks, :])
        if masked:
            rr = lax.broadcasted_iota(jnp.int32, (tq, tq), 0)
            cc = lax.broadcasted_iota(jnp.int32, (tq, tq), 1)
            s = jnp.where(cc <= rr, s, NEG_INF)
        m_new = jnp.maximum(m, jnp.max(s, axis=-1, keepdims=True))
        alpha = jnp.exp(m - m_new)
        p = jnp.exp(s - m_new)
        l = alpha * l + jnp.sum(p, axis=-1, keepdims=True)
        acc = alpha * acc + _dot(p.astype(BF16), vb_ref[ks, :])
        return m_new, l, acc

    init = (jnp.full((tq, 1), NEG_INF, F32), jnp.zeros((tq, 1), F32), jnp.zeros((tq, DH_C), F32))
    for n in range(S // tq):

        @pl.when(qi == n)
        def _(n=n):
            carry = block(n, init, True)
            for j in range(n):
                carry = block(j, carry, False)
            m, l, acc = carry
            o_ref[0] = (acc / l).astype(o_ref.dtype)


def _fox_prompt(q, k, v, c, *, tq=512):
    B, S, _ = q.shape
    tq = min(tq, S)
    assert S % tq == 0
    kv_spec = pl.BlockSpec((1, S, DH_C), lambda b, h, i: (b, 0, h))
    return pl.pallas_call(
        functools.partial(_fox_prompt_kernel, tq=tq),
        grid=(B, H_C, S // tq),
        in_specs=[pl.BlockSpec((1, tq, DH_C), lambda b, h, i: (b, i, h)), kv_spec, kv_spec,
                  pl.BlockSpec((1, S, LANES), lambda b, h, i: (b, 0, 0))],
        out_specs=pl.BlockSpec((1, tq, DH_C), lambda b, h, i: (b, i, h)),
        out_shape=jax.ShapeDtypeStruct((B, S, D_C), BF16),
        scratch_shapes=[pltpu.VMEM((S, 2 * DH_C), BF16), pltpu.VMEM((S, DH_C), BF16)],
        compiler_params=_cparams(("parallel", "parallel", "arbitrary")),
        name="fox_prompt",
    )(q, k, v, c)


def _fox_suffix_kernel(pt_ref, lf_ref, usel_ref, tsel_ref, esel_ref, o_ref, g_ref, *, n_pages):
    b = pl.program_id(0)
    for p in range(n_pages):
        g_ref[p:p + 1, :] = lf_ref[pl.ds(pt_ref[b, p], 1), :]
    g = g_ref[...]
    within = _dot01_rhs(g, usel_ref[...])
    tot = _dot01_rhs(g, tsel_ref[...])
    r = lax.broadcasted_iota(jnp.int32, (n_pages, n_pages), 0)
    s = lax.broadcasted_iota(jnp.int32, (n_pages, n_pages), 1)
    later = jnp.where(s > r, 1.0, 0.0).astype(BF16)
    carry = _dot01(later, tot)
    suf = within + _dot01_rhs(carry, esel_ref[...])
    for h in range(H_C):
        o_ref[0, h] = suf[:, h * PAGE_SIZE:(h + 1) * PAGE_SIZE]


def _dot01_rhs(x, sel):
    hi = x.astype(BF16)
    r1 = x - hi.astype(F32)
    mid = r1.astype(BF16)
    lo = (r1 - mid.astype(F32)).astype(BF16)
    return _dot(hi, sel) + _dot(mid, sel) + _dot(lo, sel)


def _fox_suffix(page_table, cache_logf_l):
    B, n_pages = page_table.shape
    n_pool, W = cache_logf_l.shape
    key_in = jnp.arange(W) // H_C
    head_in = jnp.arange(W) % H_C
    head_out = jnp.arange(W) // PAGE_SIZE
    key_out = jnp.arange(W) % PAGE_SIZE
    usel = ((head_in[:, None] == head_out[None, :]) & (key_in[:, None] > key_out[None, :])).astype(BF16)
    tsel = (head_in[:, None] == jnp.arange(LANES)[None, :]).astype(BF16)
    esel = (jnp.arange(LANES)[:, None] == head_out[None, :]).astype(BF16)
    full = lambda shp: pl.BlockSpec(shp, lambda b, pt: tuple(0 for _ in shp))
    grid_spec = pltpu.PrefetchScalarGridSpec(
        num_scalar_prefetch=1,
        grid=(B,),
        in_specs=[full((n_pool, W)), full((W, W)), full((W, LANES)), full((LANES, W))],
        out_specs=pl.BlockSpec((1, H_C, n_pages, PAGE_SIZE), lambda b, pt: (b, 0, 0, 0)),
        scratch_shapes=[pltpu.VMEM((n_pages, W), F32)],
    )
    return pl.pallas_call(
        functools.partial(_fox_suffix_kernel, n_pages=n_pages),
        grid_spec=grid_spec,
        out_shape=jax.ShapeDtypeStruct((B, H_C, n_pages, PAGE_SIZE), F32),
        compiler_params=_cparams(("arbitrary",)),
        name="fox_suffix",
    )(page_table, cache_logf_l, usel, tsel, esel)


def _fox_sample_kernel(pt_ref, q_ref, kn_ref, vn_ref, cq_ref, ckn_ref, suf_ref, *rest, t_new, ppg):
    k_refs = rest[:ppg]
    v_refs = rest[ppg:2 * ppg]
    o_ref, m_ref, l_ref, acc_ref = rest[2 * ppg:]
    g = pl.program_id(1)
    ng = pl.num_programs(1)
    scale = DH_C ** -0.5
    R = H_C * t_new

    def q_head(h):
        return q_ref[0, :, h * DH_C:(h + 1) * DH_C]

    def update(s, v_of_head):
        m = m_ref[...]
        m_new = jnp.maximum(m, jnp.max(s, axis=-1, keepdims=True))
        alpha = jnp.exp(m - m_new)
        p = jnp.exp(s - m_new)
        l_ref[...] = alpha * l_ref[...] + jnp.sum(p, axis=-1, keepdims=True)
        m_ref[...] = m_new
        for h in range(H_C):
            rows = slice(h * t_new, (h + 1) * t_new)
            acc_ref[rows, :] = alpha[rows] * acc_ref[rows, :] + _dot(p[rows].astype(BF16), v_of_head(h))

    @pl.when(g == 0)
    def _():
        m_ref[...] = jnp.full_like(m_ref, NEG_INF)
        l_ref[...] = jnp.zeros_like(l_ref)
        acc_ref[...] = jnp.zeros_like(acc_ref)
        s = jnp.concatenate(
            [_dot_nt(q_head(h), kn_ref[0, :, h * DH_C:(h + 1) * DH_C]) for h in range(H_C)], axis=0) * scale
        s = s + (cq_ref[0] - ckn_ref[0])
        qpos = lax.broadcasted_iota(jnp.int32, s.shape, 0) % t_new
        kpos = lax.broadcasted_iota(jnp.int32, s.shape, 1)
        s = jnp.where(kpos <= qpos, s, NEG_INF)
        update(s, lambda h: vn_ref[0, :, h * DH_C:(h + 1) * DH_C])

    def head_rows(ref, h):
        return ref[0, 0, pl.ds(h, PAGE_SIZE, stride=H_C), :].astype(BF16)

    for j in range(ppg):
        kj, vj = k_refs[j], v_refs[j]
        s = jnp.concatenate([_dot_nt(q_head(h), head_rows(kj, h)) for h in range(H_C)], axis=0) * scale
        bias = jnp.concatenate(
            [jnp.broadcast_to(suf_ref[0, h, pl.ds(g * ppg + j, 1), :], (t_new, PAGE_SIZE)) for h in range(H_C)], axis=0)
        s = s + (cq_ref[0] + bias)
        update(s, lambda h: head_rows(vj, h))

    @pl.when(g == ng - 1)
    def _():
        o = acc_ref[...] / l_ref[...]
        for h in range(H_C):
            o_ref[0, :, h * DH_C:(h + 1) * DH_C] = o[h * t_new:(h + 1) * t_new].astype(o_ref.dtype)


def _fox_sample(q, k_new, v_new, cq_col, ck_new, suffix, cache_k, cache_v, page_table, *, layer, t_new, ppg=8):
    B, n_pages = page_table.shape
    T = q.shape[1]
    T16 = k_new.shape[1]
    R = H_C * t_new
    assert n_pages % ppg == 0

    def page_spec(j):
        return pl.BlockSpec((1, 1, PAGE_SIZE * H_C, DH_C),
                            lambda b, g, pt: (layer, pt[b, g * ppg + j], 0, 0))

    bspec = lambda shp: pl.BlockSpec((1,) + shp, lambda b, g, pt: (b,) + tuple(0 for _ in shp))
    grid_spec = pltpu.PrefetchScalarGridSpec(
        num_scalar_prefetch=1,
        grid=(B, n_pages // ppg),
        in_specs=[bspec((T, D_C)), bspec((T16, D_C)), bspec((T16, D_C)), bspec((R, 1)), bspec((R, T16)),
                  bspec((H_C, n_pages, PAGE_SIZE))]
        + [page_spec(j) for j in range(ppg)] + [page_spec(j) for j in range(ppg)],
        out_specs=bspec((T, D_C)),
        scratch_shapes=[pltpu.VMEM((R, 1), F32), pltpu.VMEM((R, 1), F32), pltpu.VMEM((R, DH_C), F32)],
    )
    return pl.pallas_call(
        functools.partial(_fox_sample_kernel, t_new=t_new, ppg=ppg),
        grid_spec=grid_spec,
        out_shape=jax.ShapeDtypeStruct((B, T, D_C), BF16),
        compiler_params=_cparams(("parallel", "arbitrary")),
        name="fox_sample",
    )(page_table, q, k_new, v_new, cq_col, ck_new, suffix, *([cache_k] * ppg), *([cache_v] * ppg))


def _merge_kernel(ya_ref, yb_ref, yc_ref, wa_ref, wb_ref, wc_ref, ga_ref, gb_ref, gc_ref, o_ref):
    m = ga_ref[...].astype(F32) * _dot(ya_ref[...], wa_ref[...])
    m = m + gb_ref[...].astype(F32) * _dot(yb_ref[...], wb_ref[...])
    m = m + gc_ref[...].astype(F32) * _dot(yc_ref[...], wc_ref[...])
    o_ref[...] = m.astype(o_ref.dtype)


def _merge(y_a, y_b, y_c, w_a, w_b, w_c, gates, *, tm=1024, tn=512):
    M = y_a.shape[0]
    tm = min(tm, M)
    nj = D_MODEL // tn
    act = lambda K: pl.BlockSpec((tm, K), lambda i, j: (i, 0))
    wsp = lambda K: pl.BlockSpec((K, tn), lambda i, j: (0, j))
    gsp = lambda o: pl.BlockSpec((tm, tn), lambda i, j: (i, j + o * nj))
    return pl.pallas_call(
        _merge_kernel,
        grid=(M // tm, nj),
        in_specs=[act(D_A), act(D_B), act(D_C), wsp(D_A), wsp(D_B), wsp(D_C), gsp(0), gsp(1), gsp(2)],
        out_specs=pl.BlockSpec((tm, tn), lambda i, j: (i, j)),
        out_shape=jax.ShapeDtypeStruct((M, D_MODEL), BF16),
        compiler_params=_cparams(("parallel", "parallel")),
        name="merge",
    )(y_a, y_b, y_c, w_a, w_b, w_c, gates, gates, gates)


def _ln_kernel(*refs, n_delta):
    x_ref = refs[0]
    d_refs = refs[1:1 + n_delta]
    g_ref, b_ref, o_ref, ob_ref = refs[1 + n_delta:]
    y = DN_ALPHA * x_ref[...]
    for d in d_refs:
        y = y + d[...].astype(F32)
    mu = jnp.mean(y, axis=-1, keepdims=True)
    yc = y - mu
    var = jnp.mean(jnp.square(yc), axis=-1, keepdims=True)
    o = yc * lax.rsqrt(var + LN_EPS) * g_ref[...] + b_ref[...]
    o_ref[...] = o
    ob_ref[...] = o.astype(BF16)


def _res_ln(x, deltas, g, b, *, tm=128):
    M = x.shape[0]
    tm = min(tm, M)
    row = pl.BlockSpec((tm, D_MODEL), lambda i: (i, 0))
    vec = pl.BlockSpec((1, D_MODEL), lambda i: (0, 0))
    return pl.pallas_call(
        functools.partial(_ln_kernel, n_delta=len(deltas)),
        grid=(M // tm,),
        in_specs=[row] * (1 + len(deltas)) + [vec, vec],
        out_specs=[row, row],
        out_shape=[jax.ShapeDtypeStruct((M, D_MODEL), F32), jax.ShapeDtypeStruct((M, D_MODEL), BF16)],
        compiler_params=_cparams(("parallel",)),
        name="res_ln",
    )(x, *deltas, g, b)


OFF_A = 0
OFF_B = D_A
OFF_QC = OFF_B + 4 * D_B
OFF_KC = OFF_QC + D_C
OFF_VC = OFF_KC + D_C
OFF_FC = OFF_VC + D_C
N_GATES = 3 * D_MODEL


def _in_proj(xpb, xsb, w_in_t, layer):
    seg = lambda **kw: _proj(xpb, xsb, w_in_t, layer, transposed=True, **kw)
    parts = {
        "u_a": seg(n_off=OFF_A, n_out=D_A),
        "zb": seg(n_off=OFF_B, n_out=4 * D_B),
        "q_c": seg(n_off=OFF_QC, n_out=D_C, out_dtype=BF16),
        "k_c": seg(n_off=OFF_KC, n_out=D_C),
        "v_c": seg(n_off=OFF_VC, n_out=D_C),
        "f_c": seg(n_off=OFF_FC, n_out=LANES, tn=LANES),
        "gates": seg(n_off=OFF_FC + H_C, n_out=N_GATES, out_dtype=BF16, act="sigmoid"),
    }
    return ({k: v[0] for k, v in parts.items()}, {k: v[1] for k, v in parts.items()})


def _mixers(z, lw, *, layer, B, T, pool_buf16, pos0, hgrn_s0, lb_raw, sample=None):
    M = B * T
    tm = min(PROJ_TM, M)
    u_a, zb, q_c, k_c, v_c, f_c, gates = (z[k] for k in ("u_a", "zb", "q_c", "k_c", "v_c", "f_c", "gates"))

    y_a, nbuf = _pool_mix(u_a.reshape(B, T, D_A), pool_buf16, lw["w_pool"], lw["pool_scale"], pos0=pos0, tt=256)
    new_buf = nbuf[:, 1:]

    zb3 = zb.reshape(B, T, 4 * D_B)
    if T % HGRN_CHUNK:
        Tp = -(-T // HGRN_CHUNK) * HGRN_CHUNK
        zb3 = jnp.pad(zb3, ((0, 0), (0, Tp - T), (0, 0)))
        y_b, new_s = _hgrn2(zb3, lb_raw, hgrn_s0, lw["hgrn_norm_g"], layer=layer, tt=128, t_valid=T)
        y_b = y_b[:, :T]
    else:
        y_b, new_s = _hgrn2(zb3, lb_raw, hgrn_s0, lw["hgrn_norm_g"], layer=layer, tt=128)

    q3, k3, v3 = (a.reshape(B, T, D_C) for a in (q_c, k_c, v_c))
    if sample is None:
        logf, c = _fox_gate(f_c.reshape(B, T, LANES), lw["fox_fb"])
        y_c = _fox_prompt(q3, k3, v3, c)
    else:
        T16 = PAGE_SIZE
        f3 = jnp.pad(f_c.reshape(B, T, LANES), ((0, 0), (0, T16 - T), (0, 0)))
        logf, c = _fox_gate(f3, lw["fox_fb"], t_valid=T)
        logf = logf[:, :T]
        c_hm = jnp.transpose(c[:, :, :H_C], (0, 2, 1))
        cq_col = c_hm[:, :, :T].reshape(B, H_C * T, 1)
        ck_new = jnp.repeat(c_hm, T, axis=1)
        pad_kv = lambda a: jnp.pad(a.astype(BF16), ((0, 0), (0, T16 - T), (0, 0)))
        suffix = _fox_suffix(sample["page_table"], sample["cache_logf"][layer])
        y_c = _fox_sample(q3, pad_kv(k3), pad_kv(v3), cq_col, ck_new, suffix, sample["cache_k"], sample["cache_v"],
                          sample["page_table"], layer=layer, t_new=T)
    logf = logf[:, :, :H_C]

    merged = _merge(y_a.reshape(M, D_A), y_b.reshape(M, D_B), y_c.reshape(M, D_C),
                    lw["w_up_a"], lw["w_up_b"], lw["w_up_c"], gates, tm=tm)
    return merged, new_buf, new_s, logf


def kernel(x_prompt, x_sample, cache_k, cache_v, cache_logf, state_hgrn, state_pool, page_table, p_prompt, p_sample,
           w_in, fox_fb, w_pool, pool_scale, hgrn_lb, hgrn_norm_g, w_up_a, w_up_b, w_up_c, w_out, ln1_g, ln1_b,
           w_ff_up, w_ff_down, w_ple, w_ple_gate, ln2_g, ln2_b):
    Bp, Tp, _ = x_prompt.shape
    Bs, Ts, _ = x_sample.shape
    n_pool = cache_k.shape[1]
    past_len = page_table.shape[1] * PAGE_SIZE
    cache_logf2 = cache_logf.reshape(DEPTH, n_pool, PAGE_SIZE * H_C)
    page_rows = (DEPTH, n_pool, PAGE_SIZE * H_C, DH_C)
    sample_ctx = {"page_table": page_table, "cache_k": cache_k.reshape(page_rows),
                  "cache_v": cache_v.reshape(page_rows), "cache_logf": cache_logf2}

    xp = x_prompt.reshape(Bp * Tp, D_MODEL)
    xs = x_sample.reshape(Bs * Ts, D_MODEL)
    xpb, xsb = xp.astype(BF16), xs.astype(BF16)
    zero_buf = jnp.zeros((Bp, POOL_HALO, D_A), F32)
    zero_state = jnp.zeros((Bp, H_B, DK_B, DK_B), F32)
    lb_raw = hgrn_lb.astype(F32)
    w_in_t = jnp.swapaxes(w_in, 1, 2)
    w_down = w_ff_down.astype(BF16)

    outs = [[] for _ in range(10)]
    for i in range(DEPTH):
        lw = {
            "fox_fb": jnp.pad(fox_fb[i].astype(F32), (0, LANES - H_C)).reshape(1, LANES),
            "w_pool": w_pool[i].astype(BF16),
            "pool_scale": pool_scale[i].astype(F32).reshape(1, D_A),
            "hgrn_norm_g": hgrn_norm_g[i].astype(F32).reshape(1, DK_B),
            "w_up_a": w_up_a[i].astype(BF16), "w_up_b": w_up_b[i].astype(BF16), "w_up_c": w_up_c[i].astype(BF16),
        }
        vec = lambda a: a[i].astype(F32).reshape(1, D_MODEL)
        pp = p_prompt[i].reshape(Bp * Tp, -1).astype(BF16)
        ps = p_sample[i].reshape(Bs * Ts, -1).astype(BF16)

        zp, zs = _in_proj(xpb, xsb, w_in_t, i)
        merged_p, buf_p, s_p, f_p = _mixers(
            zp, lw, layer=i, B=Bp, T=Tp, pool_buf16=zero_buf, pos0=0, hgrn_s0=zero_state, lb_raw=lb_raw)
        buf16 = jnp.pad(state_pool[i], ((0, 0), (POOL_HALO - POOL_BUF, 0), (0, 0)))
        merged_s, buf_s, s_s, f_s = _mixers(
            zs, lw, layer=i, B=Bs, T=Ts, pool_buf16=buf16, pos0=past_len, hgrn_s0=state_hgrn[i], lb_raw=lb_raw,
            sample=sample_ctx)

        mix_p, mix_s = _proj(merged_p, merged_s, w_out, i, out_dtype=BF16)
        hp, hpb = _res_ln(xp, [mix_p], vec(ln1_g), vec(ln1_b))
        hs, hsb = _res_ln(xs, [mix_s], vec(ln1_g), vec(ln1_b))
        up_p, up_s = _proj(hpb, hsb, w_ff_up, i, out_dtype=BF16, act="relu2")
        ff_p = _mm(up_p, w_down, i)
        ff_s = _mm(up_s, w_down, i)
        ple_p, ple_s = _proj(hpb, hsb, w_ple_gate, i, act="sigmoid", side=(pp, ps, w_ple), out_dtype=BF16)
        xp, xpb = _res_ln(hp, [ff_p, ple_p], vec(ln2_g), vec(ln2_b))
        xs, xsb = _res_ln(hs, [ff_s, ple_s], vec(ln2_g), vec(ln2_b))

        vals = (zp["k_c"].reshape(Bp, Tp, H_C, DH_C), zp["v_c"].reshape(Bp, Tp, H_C, DH_C), f_p,
                zs["k_c"].reshape(Bs, Ts, H_C, DH_C), zs["v_c"].reshape(Bs, Ts, H_C, DH_C), f_s,
                s_p, s_s, buf_p, buf_s)
        for o, val in zip(outs, vals):
            o.append(val)
    return (xp.reshape(Bp, Tp, D_MODEL), xs.reshape(Bs, Ts, D_MODEL)) + tuple(jnp.stack(o) for o in outs)
```

```python
import functools
import math

import jax
import jax.numpy as jnp
from jax import lax
from jax.experimental import pallas as pl
from jax.experimental.pallas import tpu as pltpu

F32 = jnp.float32
BF16 = jnp.bfloat16

D_MODEL = 4096
DEPTH = 2
PAGE_SIZE = 128
POOL_WINDOWS = (2, 4, 8, 16)
D_A = D_MODEL // 4
G_A = D_A // len(POOL_WINDOWS)
POOL_BUF = 15
D_B = D_MODEL // 4
DK_B = 128
H_B = D_B // DK_B
D_C = D_MODEL // 2
DH_C = 128
H_C = D_C // DH_C
D_FF = 4 * D_MODEL
DN_ALPHA = (2 * DEPTH) ** 0.25
LN_EPS = 1e-5
RMS_EPS = 1e-6

LANES = 128
SUBLANES = 8
VMEM_LIMIT = 58 * 1024 * 1024

HGRN_CHUNK = 16
HGRN_ROWS = 32
POOL_HALO = 16

NEG_INF = float("-inf")


def _cparams(sem):
    return pltpu.CompilerParams(dimension_semantics=sem, vmem_limit_bytes=VMEM_LIMIT)


def _dot(a, b):
    return jnp.dot(a, b, preferred_element_type=F32)


def _dot_nt(a, b):
    return lax.dot_general(a, b, (((1,), (1,)), ((), ())), preferred_element_type=F32)


def _dot_tn(a, b):
    return lax.dot_general(a, b, (((0,), (0,)), ((), ())), preferred_element_type=F32)


def _dot01(sel, x):
    hi = x.astype(BF16)
    r1 = x - hi.astype(F32)
    mid = r1.astype(BF16)
    lo = (r1 - mid.astype(F32)).astype(BF16)
    return _dot(sel, hi) + _dot(sel, mid) + _dot(sel, lo)


def _log_sigmoid(z):
    return jnp.minimum(z, 0.0) - jnp.log1p(jnp.exp(-jnp.abs(z)))


def _sigmoid(z):
    return 0.5 * jnp.tanh(0.5 * z) + 0.5


def _mm_kernel(x_ref, w_ref, o_ref, acc_ref):
    k = pl.program_id(2)

    @pl.when(k == 0)
    def _():
        acc_ref[...] = jnp.zeros_like(acc_ref)

    acc_ref[...] += _dot(x_ref[...], w_ref[...])

    @pl.when(k == pl.num_programs(2) - 1)
    def _():
        o_ref[...] = acc_ref[...].astype(o_ref.dtype)


def _mm(x, w, layer, *, out_dtype=BF16, tm=1024, tn=1024, tk=4096):
    M, K = x.shape
    N = w.shape[2]
    tm, tn, tk = min(tm, M), min(tn, N), min(tk, K)
    assert M % tm == 0 and N % tn == 0 and K % tk == 0
    return pl.pallas_call(
        _mm_kernel,
        grid=(M // tm, N // tn, K // tk),
        in_specs=[pl.BlockSpec((tm, tk), lambda i, j, k: (i, k)),
                  pl.BlockSpec((None, tk, tn), lambda i, j, k: (layer, k, j))],
        out_specs=pl.BlockSpec((tm, tn), lambda i, j, k: (i, j)),
        out_shape=jax.ShapeDtypeStruct((M, N), out_dtype),
        scratch_shapes=[pltpu.VMEM((tm, tn), F32)],
        compiler_params=_cparams(("parallel", "parallel", "arbitrary")),
        name="dense",
    )(x, w)


PROJ_TM = 1024
PROJ_TN = 1024
CAST_ROWS = 256


def _proj_kernel(*refs, act, has_side, transposed, layer, start, tn):
    refs = list(refs)
    xp_ref, xs_ref, w_hbm = refs[:3]
    del refs[:3]
    if has_side:
        pp_ref, ps_ref, w2_ref = refs[:3]
        del refs[:3]
    op_ref, os_ref, wf32_ref, wbf_ref, sem = refs[:5]
    w2bf_ref = refs[5] if has_side else None
    j, i = pl.program_id(0), pl.program_id(1)
    nj = pl.num_programs(0)
    n_rows = wbf_ref.shape[0]
    mm = _dot_nt if transposed else _dot

    def tile_copy(jj):
        cols = pl.ds(pl.multiple_of(start + jj * tn, math.gcd(start, tn)), tn)
        src = w_hbm.at[layer, cols, :] if transposed else w_hbm.at[layer, :, cols]
        return pltpu.make_async_copy(src, wf32_ref, sem)

    def finish(acc, p_ref, o_ref):
        if act == "sigmoid":
            acc = _sigmoid(acc)
        elif act == "relu2":
            acc = jnp.square(jnp.maximum(acc, 0.0))
        if has_side:
            acc = _dot(p_ref[...], w2bf_ref[...]) * acc
        o_ref[...] = acc.astype(o_ref.dtype)

    @pl.when(i == 0)
    def _():
        @pl.when(j == 0)
        def _():
            tile_copy(0).start()

        tile_copy(j).wait()
        for r in range(0, n_rows, CAST_ROWS):
            n = min(CAST_ROWS, n_rows - r)
            wbf_ref[r:r + n, :] = wf32_ref[r:r + n, :].astype(BF16)
        if has_side:
            w2bf_ref[...] = w2_ref[...].astype(BF16)
        finish(mm(xs_ref[...], wbf_ref[...]), ps_ref if has_side else None, os_ref)

    @pl.when((i == 1) & (j + 1 < nj))
    def _():
        tile_copy(j + 1).start()

    @pl.when(i > 0)
    def _():
        finish(mm(xp_ref[...], wbf_ref[...]), pp_ref if has_side else None, op_ref)


def _proj(xp, xs, w, layer, *, n_off=0, n_out=None, out_dtype=F32, act=None, side=None, transposed=False,
          tm=PROJ_TM, tn=PROJ_TN):
    Mp, K = xp.shape
    Ms = xs.shape[0]
    n_all = w.shape[1] if transposed else w.shape[2]
    n_out = n_all - n_off if n_out is None else n_out
    tm = min(tm, Mp)
    assert Mp % tm == 0 and n_out % tn == 0
    assert n_off % (SUBLANES if transposed else LANES) == 0
    nmp = Mp // tm
    row = lambda i: jnp.maximum(i - 1, 0)
    in_specs = [pl.BlockSpec((tm, K), lambda j, i: (row(i), 0)), pl.BlockSpec((Ms, K), lambda j, i: (0, 0)),
                pl.BlockSpec(memory_space=pl.ANY)]
    args = [xp, xs, w]
    w_tile = (tn, K) if transposed else (K, tn)
    scratch = [pltpu.VMEM(w_tile, F32), pltpu.VMEM(w_tile, BF16), pltpu.SemaphoreType.DMA]
    if side is not None:
        pp, ps, w2 = side
        K2 = pp.shape[1]
        in_specs += [pl.BlockSpec((tm, K2), lambda j, i: (row(i), 0)),
                     pl.BlockSpec((Ms, K2), lambda j, i: (0, 0)),
                     pl.BlockSpec((None, K2, tn), lambda j, i: (layer, 0, j))]
        args += [pp, ps, w2]
        scratch.append(pltpu.VMEM((K2, tn), BF16))
    return pl.pallas_call(
        functools.partial(_proj_kernel, act=act, has_side=side is not None, transposed=transposed, layer=layer,
                          start=n_off, tn=tn),
        grid=(n_out // tn, nmp + 1),
        in_specs=in_specs,
        out_specs=[pl.BlockSpec((tm, tn), lambda j, i: (row(i), j)),
                   pl.BlockSpec((Ms, tn), lambda j, i: (0, j))],
        out_shape=[jax.ShapeDtypeStruct((Mp, n_out), out_dtype), jax.ShapeDtypeStruct((Ms, n_out), out_dtype)],
        scratch_shapes=scratch,
        compiler_params=_cparams(("arbitrary", "arbitrary")),
        name="proj",
    )(*args)


def _pool_kernel(u_ref, buf_ref, w_ref, scale_ref, y_ref, nbuf_ref, ext_ref, *, tt, pos0):
    t = pl.program_id(1)

    @pl.when(t == 0)
    def _():
        ext_ref[0:POOL_HALO, :] = buf_ref[0]

    u = u_ref[0]
    ext_ref[POOL_HALO:POOL_HALO + tt, :] = u
    pos = pos0 + t * tt + lax.broadcasted_iota(jnp.int32, (tt, 1), 0)
    for gi, w in enumerate(POOL_WINDOWS):
        cols = slice(gi * G_A, (gi + 1) * G_A)
        win = u[:, cols]
        for j in range(1, w):
            win = win + ext_ref[POOL_HALO - j:POOL_HALO - j + tt, cols]
        cnt = jnp.minimum(pos + 1, w).astype(F32)
        pooled = win / cnt - u[:, cols]
        y = _dot(pooled.astype(BF16), w_ref[gi]) * scale_ref[:, cols]
        y_ref[0, :, cols] = y.astype(y_ref.dtype)
    tail = ext_ref[tt:tt + POOL_HALO, :]
    nbuf_ref[0] = tail
    ext_ref[0:POOL_HALO, :] = tail


def _pool_mix(u, buf16, w_pool, scale, *, pos0, tt):
    B, T, _ = u.shape
    tt = min(tt, T)
    assert T % tt == 0
    return pl.pallas_call(
        functools.partial(_pool_kernel, tt=tt, pos0=pos0),
        grid=(B, T // tt),
        in_specs=[pl.BlockSpec((1, tt, D_A), lambda b, t: (b, t, 0)),
                  pl.BlockSpec((1, POOL_HALO, D_A), lambda b, t: (b, 0, 0)),
                  pl.BlockSpec((len(POOL_WINDOWS), G_A, G_A), lambda b, t: (0, 0, 0)),
                  pl.BlockSpec((1, D_A), lambda b, t: (0, 0))],
        out_specs=[pl.BlockSpec((1, tt, D_A), lambda b, t: (b, t, 0)),
                   pl.BlockSpec((1, POOL_HALO, D_A), lambda b, t: (b, 0, 0))],
        out_shape=[jax.ShapeDtypeStruct((B, T, D_A), BF16),
                   jax.ShapeDtypeStruct((B, POOL_HALO, D_A), F32)],
        scratch_shapes=[pltpu.VMEM((POOL_HALO + tt, D_A), F32)],
        compiler_params=_cparams(("parallel", "arbitrary")),
        name="pool_mix",
    )(u, buf16, w_pool, scale)


def _hgrn_kernel(q_ref, f_ref, i_ref, g_ref, lbraw_ref, s0_ref, ng_ref, y_ref, sfin_ref,
                 st_ref, b_ref, kk_ref, v_ref, qd_ref, o_ref, *, layer, tt, t_valid):
    t = pl.program_id(1)
    nt = pl.num_programs(1)
    C = HGRN_CHUNK

    @pl.when(t == 0)
    def _():
        for h in range(H_B):
            st_ref[h] = s0_ref[0, h].T

    raw = lbraw_ref[...]
    e = jnp.exp(raw - jnp.max(raw, axis=0, keepdims=True))
    sm = e / jnp.sum(e, axis=0, keepdims=True)
    lb = jnp.zeros((1, D_B), F32)
    for j in range(1, layer + 1):
        lb = lb + sm[j:j + 1]
    lb = (lb + sm[0:1]) - sm[0:1]

    z = f_ref[0]
    la = jnp.log(lb)
    c = jnp.log1p(-lb) + _log_sigmoid(z)
    amax = jnp.maximum(la, c)
    delta = la - c
    lf = jnp.where(jnp.isnan(delta), la + c, amax + jnp.log1p(jnp.exp(-jnp.abs(delta))))
    kk = (1.0 - lb) * _sigmoid(-z)
    if t_valid is not None:
        row = t * tt + lax.broadcasted_iota(jnp.int32, (tt, 1), 0)
        lf = jnp.where(row < t_valid, lf, 0.0)
        kk = jnp.where(row < t_valid, kk, 0.0)
    r = lax.broadcasted_iota(jnp.int32, (tt, tt), 0)
    s = lax.broadcasted_iota(jnp.int32, (tt, tt), 1)
    tri = jnp.where((s <= r) & (s >= jnp.bitwise_and(r, -C)), 1.0, 0.0).astype(BF16)
    b_all = _dot01(tri, lf)
    qs_all = q_ref[0] * DK_B ** -0.5
    v_all = i_ref[0]
    heads = [slice(h * DK_B, (h + 1) * DK_B) for h in range(H_B)]
    halo = jnp.zeros((C, DK_B), F32)
    for h, hl in enumerate(heads):
        for ref, val in ((b_ref, b_all), (kk_ref, kk), (v_ref, v_all)):
            ref[h, 0:C, :] = halo
            ref[h, C:C + tt, :] = val[:, hl]

    nr = min(tt, HGRN_ROWS)
    tmod = jnp.bitwise_and(lax.broadcasted_iota(jnp.int32, (nr, 1), 0), C - 1)
    for h, hl in enumerate(heads):
        for r0 in range(0, tt, nr):
            rows = slice(r0, r0 + nr)
            bh, qh = b_all[rows, hl], qs_all[rows, hl]
            cur = slice(C + r0, C + r0 + nr)
            acc = jnp.sum(qh * kk_ref[h, cur, :], axis=-1, keepdims=True) * v_ref[h, cur, :]
            for d in range(1, C):
                lag = slice(C + r0 - d, C + r0 - d + nr)
                w = qh * jnp.exp(jnp.minimum(bh - b_ref[h, lag, :], 0.0)) * kk_ref[h, lag, :]
                rs = jnp.sum(w, axis=-1, keepdims=True)
                acc = acc + jnp.where(tmod >= d, rs, 0.0) * v_ref[h, lag, :]
            o_ref[rows, hl] = acc
            qd_ref[rows, hl] = qh * jnp.exp(bh)

    def chunk(ci, carry):
        rows = pl.ds(pl.multiple_of(ci * C, C), C)
        rows_h = pl.ds(pl.multiple_of(ci * C + C, C), C)
        for h, hl in enumerate(heads):
            b = b_ref[h, rows_h, :]
            v = v_ref[h, rows_h, :]
            st = st_ref[h]
            o_ref[rows, hl] += _dot_nt(qd_ref[rows, hl].astype(BF16), st.astype(BF16))
            b_last = b[C - 1:C]
            k_dec = kk_ref[h, rows_h, :] * jnp.exp(b_last - b)
            st_ref[h] = st * jnp.exp(b_last) + _dot_tn(v.astype(BF16), k_dec.astype(BF16))
        return carry

    n_chunks = tt // C
    lax.fori_loop(0, n_chunks, chunk, 0, unroll=math.gcd(n_chunks, 8))

    for h, hl in enumerate(heads):
        o = o_ref[:, hl]
        o = o * lax.rsqrt(jnp.mean(jnp.square(o), axis=-1, keepdims=True) + RMS_EPS) * ng_ref[...]
        o = o * _sigmoid(g_ref[0, :, hl])
        y_ref[0, :, hl] = o.astype(y_ref.dtype)

    @pl.when(t == nt - 1)
    def _():
        for h in range(H_B):
            sfin_ref[0, h] = st_ref[h].T


def _hgrn2(zb, lb_raw, s0, norm_g, *, layer, tt, t_valid=None):
    B, T, _ = zb.shape
    tt = min(tt, T)
    assert T % tt == 0 and tt % HGRN_CHUNK == 0

    def col(j):
        return pl.BlockSpec((1, tt, D_B), lambda b, t: (b, t, j))

    return pl.pallas_call(
        functools.partial(_hgrn_kernel, layer=layer, tt=tt, t_valid=t_valid),
        grid=(B, T // tt),
        in_specs=[col(0), col(1), col(2), col(3),
                  pl.BlockSpec((DEPTH, D_B), lambda b, t: (0, 0)),
                  pl.BlockSpec((1, H_B, DK_B, DK_B), lambda b, t: (b, 0, 0, 0)),
                  pl.BlockSpec((1, DK_B), lambda b, t: (0, 0))],
        out_specs=[pl.BlockSpec((1, tt, D_B), lambda b, t: (b, t, 0)),
                   pl.BlockSpec((1, H_B, DK_B, DK_B), lambda b, t: (b, 0, 0, 0))],
        out_shape=[jax.ShapeDtypeStruct((B, T, D_B), BF16),
                   jax.ShapeDtypeStruct((B, H_B, DK_B, DK_B), F32)],
        scratch_shapes=[pltpu.VMEM((H_B, DK_B, DK_B), F32)]
        + [pltpu.VMEM((H_B, HGRN_CHUNK + tt, DK_B), F32)] * 3
        + [pltpu.VMEM((tt, D_B), F32)] * 2,
        compiler_params=_cparams(("parallel", "arbitrary")),
        name="hgrn2",
    )(zb, zb, zb, zb, lb_raw, s0, norm_g)


def _fox_gate_kernel(f_ref, fb_ref, logf_ref, c_ref, *, t_valid):
    S = f_ref.shape[1]
    logf = _log_sigmoid(f_ref[0] + fb_ref[...])
    logf_ref[0] = logf
    if t_valid is not None:
        row = lax.broadcasted_iota(jnp.int32, (S, 1), 0)
        logf = jnp.where(row < t_valid, logf, 0.0)
    r = lax.broadcasted_iota(jnp.int32, (S, S), 0)
    s = lax.broadcasted_iota(jnp.int32, (S, S), 1)
    tri = jnp.where(s <= r, 1.0, 0.0).astype(BF16)
    c_ref[0] = _dot01(tri, logf)


def _fox_gate(f_pad, fb_pad, *, t_valid=None):
    B, S, L = f_pad.shape
    spec = pl.BlockSpec((1, S, L), lambda b: (b, 0, 0))
    return pl.pallas_call(
        functools.partial(_fox_gate_kernel, t_valid=t_valid),
        grid=(B,),
        in_specs=[spec, pl.BlockSpec((1, L), lambda b: (0, 0))],
        out_specs=[spec, spec],
        out_shape=[jax.ShapeDtypeStruct((B, S, L), F32)] * 2,
        compiler_params=_cparams(("parallel",)),
        name="fox_gate",
    )(f_pad, fb_pad)


def _split3(c):
    hi = c.astype(BF16).astype(F32)
    r1 = c - hi
    mid = r1.astype(BF16).astype(F32)
    lo = (r1 - mid).astype(BF16).astype(F32)
    return hi, mid, lo


def _bias_lanes(c, sign, first):
    hi, mid, lo = _split3(c)
    lane = lax.broadcasted_iota(jnp.int32, (c.shape[0], LANES), 1)
    base = 0 if first else 3
    vals = jnp.where(lane == base, hi, jnp.where(lane == base + 1, mid, jnp.where(lane == base + 2, lo, 0.0))) * sign
    ones = jnp.where((lane >= 3 - base) & (lane < 6 - base), 1.0, 0.0)
    return (vals + ones).astype(BF16)


def _fox_prompt_kernel(q_ref, k_ref, v_ref, c_ref, o_ref, kaug_ref, vb_ref, *, tq):
    h = pl.program_id(1)
    qi = pl.program_id(2)
    S = k_ref.shape[1]
    scale = DH_C ** -0.5
    lane = lax.broadcasted_iota(jnp.int32, (tq, LANES), 1)

    def head_col(rows):
        return jnp.sum(jnp.where(lane == h, c_ref[0, rows, :], 0.0), axis=-1, keepdims=True)

    @pl.when(qi == 0)
    def _():
        for r in range(0, S, tq):
            rows = slice(r, r + tq)
            kaug_ref[rows, 0:DH_C] = (k_ref[0, rows, :] * scale).astype(BF16)
            kaug_ref[rows, DH_C:] = _bias_lanes(head_col(rows), -1.0, first=False)
            vb_ref[rows, :] = v_ref[0, rows, :].astype(BF16)

    q_rows = pl.ds(pl.multiple_of(qi * tq, tq), tq)
    q = jnp.concatenate([q_ref[0], _bias_lanes(head_col(q_rows), 1.0, first=True)], axis=1)

    def block(j, carry, masked):
        m, l, acc = carry
        ks = slice(j * tq, (j + 1) * tq)
        s = _dot_nt(q, kaug_ref[ks, :])
        if masked:
            rr = lax.broadcasted_iota(jnp.int32, (tq, tq), 0)
            cc = lax.broadcasted_iota(jnp.int32, (tq, tq), 1)
            s = jnp.where(cc <= rr, s, NEG_INF)
        m_new = jnp.maximum(m, jnp.max(s, axis=-1, keepdims=True))
        alpha = jnp.exp(m - m_new)
        p = jnp.exp(s - m_new)
        l = alpha * l + jnp.sum(p, axis=-1, keepdims=True)
        acc = alpha * acc + _dot(p.astype(BF16), vb_ref[ks, :])
        return m_new, l, acc

    init = (jnp.full((tq, 1), NEG_INF, F32), jnp.zeros((tq, 1), F32), jnp.zeros((tq, DH_C), F32))
    for n in range(S // tq):

        @pl.when(qi == n)
        def _(n=n):
            carry = block(n, init, True)
            for j in range(n):
                carry = block(j, carry, False)
            m, l, acc = carry
            o_ref[0] = (acc / l).astype(o_ref.dtype)


def _fox_prompt(q, k, v, c, *, tq=512):
    B, S, _ = q.shape
    tq = min(tq, S)
    assert S % tq == 0
    kv_spec = pl.BlockSpec((1, S, DH_C), lambda b, h, i: (b, 0, h))
    return pl.pallas_call(
        functools.partial(_fox_prompt_kernel, tq=tq),
        grid=(B, H_C, S // tq),
        in_specs=[pl.BlockSpec((1, tq, DH_C), lambda b, h, i: (b, i, h)), kv_spec, kv_spec,
                  pl.BlockSpec((1, S, LANES), lambda b, h, i: (b, 0, 0))],
        out_specs=pl.BlockSpec((1, tq, DH_C), lambda b, h, i: (b, i, h)),
        out_shape=jax.ShapeDtypeStruct((B, S, D_C), BF16),
        scratch_shapes=[pltpu.VMEM((S, 2 * DH_C), BF16), pltpu.VMEM((S, DH_C), BF16)],
        compiler_params=_cparams(("parallel", "parallel", "arbitrary")),
        name="fox_prompt",
    )(q, k, v, c)


def _fox_suffix_kernel(pt_ref, lf_ref, usel_ref, tsel_ref, esel_ref, o_ref, g_ref, *, n_pages):
    b = pl.program_id(0)
    for p in range(n_pages):
        g_ref[p:p + 1, :] = lf_ref[pl.ds(pt_ref[b, p], 1), :]
    g = g_ref[...]
    within = _dot01_rhs(g, usel_ref[...])
    tot = _dot01_rhs(g, tsel_ref[...])
    r = lax.broadcasted_iota(jnp.int32, (n_pages, n_pages), 0)
    s = lax.broadcasted_iota(jnp.int32, (n_pages, n_pages), 1)
    later = jnp.where(s > r, 1.0, 0.0).astype(BF16)
    carry = _dot01(later, tot)
    suf = within + _dot01_rhs(carry, esel_ref[...])
    for h in range(H_C):
        o_ref[0, h] = suf[:, h * PAGE_SIZE:(h + 1) * PAGE_SIZE]


def _dot01_rhs(x, sel):
    hi = x.astype(BF16)
    r1 = x - hi.astype(F32)
    mid = r1.astype(BF16)
    lo = (r1 - mid.astype(F32)).astype(BF16)
    return _dot(hi, sel) + _dot(mid, sel) + _dot(lo, sel)


def _fox_suffix(page_table, cache_logf_l):
    B, n_pages = page_table.shape
    n_pool, W = cache_logf_l.shape
    key_in = jnp.arange(W) // H_C
    head_in = jnp.arange(W) % H_C
    head_out = jnp.arange(W) // PAGE_SIZE
    key_out = jnp.arange(W) % PAGE_SIZE
    usel = ((head_in[:, None] == head_out[None, :]) & (key_in[:, None] > key_out[None, :])).astype(BF16)
    tsel = (head_in[:, None] == jnp.arange(LANES)[None, :]).astype(BF16)
    esel = (jnp.arange(LANES)[:, None] == head_out[None, :]).astype(BF16)
    full = lambda shp: pl.BlockSpec(shp, lambda b, pt: tuple(0 for _ in shp))
    grid_spec = pltpu.PrefetchScalarGridSpec(
        num_scalar_prefetch=1,
        grid=(B,),
        in_specs=[full((n_pool, W)), full((W, W)), full((W, LANES)), full((LANES, W))],
        out_specs=pl.BlockSpec((1, H_C, n_pages, PAGE_SIZE), lambda b, pt: (b, 0, 0, 0)),
        scratch_shapes=[pltpu.VMEM((n_pages, W), F32)],
    )
    return pl.pallas_call(
        functools.partial(_fox_suffix_kernel, n_pages=n_pages),
        grid_spec=grid_spec,
        out_shape=jax.ShapeDtypeStruct((B, H_C, n_pages, PAGE_SIZE), F32),
        compiler_params=_cparams(("arbitrary",)),
        name="fox_suffix",
    )(page_table, cache_logf_l, usel, tsel, esel)


def _fox_sample_kernel(pt_ref, q_ref, kn_ref, vn_ref, cq_ref, ckn_ref, suf_ref, *rest, t_new, ppg):
    k_refs = rest[:ppg]
    v_refs = rest[ppg:2 * ppg]
    o_ref, m_ref, l_ref, acc_ref = rest[2 * ppg:]
    g = pl.program_id(1)
    ng = pl.num_programs(1)
    scale = DH_C ** -0.5
    R = H_C * t_new

    def q_head(h):
        return q_ref[0, :, h * DH_C:(h + 1) * DH_C]

    def update(s, v_of_head):
        m = m_ref[...]
        m_new = jnp.maximum(m, jnp.max(s, axis=-1, keepdims=True))
        alpha = jnp.exp(m - m_new)
        p = jnp.exp(s - m_new)
        l_ref[...] = alpha * l_ref[...] + jnp.sum(p, axis=-1, keepdims=True)
        m_ref[...] = m_new
        for h in range(H_C):
            rows = slice(h * t_new, (h + 1) * t_new)
            acc_ref[rows, :] = alpha[rows] * acc_ref[rows, :] + _dot(p[rows].astype(BF16), v_of_head(h))

    @pl.when(g == 0)
    def _():
        m_ref[...] = jnp.full_like(m_ref, NEG_INF)
        l_ref[...] = jnp.zeros_like(l_ref)
        acc_ref[...] = jnp.zeros_like(acc_ref)
        s = jnp.concatenate(
            [_dot_nt(q_head(h), kn_ref[0, :, h * DH_C:(h + 1) * DH_C]) for h in range(H_C)], axis=0) * scale
        s = s + (cq_ref[0] - ckn_ref[0])
        qpos = lax.broadcasted_iota(jnp.int32, s.shape, 0) % t_new
        kpos = lax.broadcasted_iota(jnp.int32, s.shape, 1)
        s = jnp.where(kpos <= qpos, s, NEG_INF)
        update(s, lambda h: vn_ref[0, :, h * DH_C:(h + 1) * DH_C])

    def head_rows(ref, h):
        return ref[0, 0, pl.ds(h, PAGE_SIZE, stride=H_C), :].astype(BF16)

    for j in range(ppg):
        kj, vj = k_refs[j], v_refs[j]
        s = jnp.concatenate([_dot_nt(q_head(h), head_rows(kj, h)) for h in range(H_C)], axis=0) * scale
        bias = jnp.concatenate(
            [jnp.broadcast_to(suf_ref[0, h, pl.ds(g * ppg + j, 1), :], (t_new, PAGE_SIZE)) for h in range(H_C)], axis=0)
        s = s + (cq_ref[0] + bias)
        update(s, lambda h: head_rows(vj, h))

    @pl.when(g == ng - 1)
    def _():
        o = acc_ref[...] / l_ref[...]
        for h in range(H_C):
            o_ref[0, :, h * DH_C:(h + 1) * DH_C] = o[h * t_new:(h + 1) * t_new].astype(o_ref.dtype)


def _fox_sample(q, k_new, v_new, cq_col, ck_new, suffix, cache_k, cache_v, page_table, *, layer, t_new, ppg=8):
    B, n_pages = page_table.shape
    T = q.shape[1]
    T16 = k_new.shape[1]
    R = H_C * t_new
    assert n_pages % ppg == 0

    def page_spec(j):
        return pl.BlockSpec((1, 1, PAGE_SIZE * H_C, DH_C),
                            lambda b, g, pt: (layer, pt[b, g * ppg + j], 0, 0))

    bspec = lambda shp: pl.BlockSpec((1,) + shp, lambda b, g, pt: (b,) + tuple(0 for _ in shp))
    grid_spec = pltpu.PrefetchScalarGridSpec(
        num_scalar_prefetch=1,
        grid=(B, n_pages // ppg),
        in_specs=[bspec((T, D_C)), bspec((T16, D_C)), bspec((T16, D_C)), bspec((R, 1)), bspec((R, T16)),
                  bspec((H_C, n_pages, PAGE_SIZE))]
        + [page_spec(j) for j in range(ppg)] + [page_spec(j) for j in range(ppg)],
        out_specs=bspec((T, D_C)),
        scratch_shapes=[pltpu.VMEM((R, 1), F32), pltpu.VMEM((R, 1), F32), pltpu.VMEM((R, DH_C), F32)],
    )
    return pl.pallas_call(
        functools.partial(_fox_sample_kernel, t_new=t_new, ppg=ppg),
        grid_spec=grid_spec,
        out_shape=jax.ShapeDtypeStruct((B, T, D_C), BF16),
        compiler_params=_cparams(("parallel", "arbitrary")),
        name="fox_sample",
    )(page_table, q, k_new, v_new, cq_col, ck_new, suffix, *([cache_k] * ppg), *([cache_v] * ppg))


def _merge_kernel(ya_ref, yb_ref, yc_ref, wa_ref, wb_ref, wc_ref, ga_ref, gb_ref, gc_ref, o_ref):
    m = ga_ref[...].astype(F32) * _dot(ya_ref[...], wa_ref[...])
    m = m + gb_ref[...].astype(F32) * _dot(yb_ref[...], wb_ref[...])
    m = m + gc_ref[...].astype(F32) * _dot(yc_ref[...], wc_ref[...])
    o_ref[...] = m.astype(o_ref.dtype)


def _merge(y_a, y_b, y_c, w_a, w_b, w_c, gates, *, tm=1024, tn=1024):
    M = y_a.shape[0]
    tm = min(tm, M)
    nj = D_MODEL // tn
    act = lambda K: pl.BlockSpec((tm, K), lambda i, j: (i, 0))
    wsp = lambda K: pl.BlockSpec((K, tn), lambda i, j: (0, j))
    gsp = lambda o: pl.BlockSpec((tm, tn), lambda i, j: (i, j + o * nj))
    return pl.pallas_call(
        _merge_kernel,
        grid=(M // tm, nj),
        in_specs=[act(D_A), act(D_B), act(D_C), wsp(D_A), wsp(D_B), wsp(D_C), gsp(0), gsp(1), gsp(2)],
        out_specs=pl.BlockSpec((tm, tn), lambda i, j: (i, j)),
        out_shape=jax.ShapeDtypeStruct((M, D_MODEL), BF16),
        compiler_params=_cparams(("parallel", "parallel")),
        name="merge",
    )(y_a, y_b, y_c, w_a, w_b, w_c, gates, gates, gates)


def _ln_kernel(*refs, n_delta):
    x_ref = refs[0]
    d_refs = refs[1:1 + n_delta]
    g_ref, b_ref, o_ref, ob_ref = refs[1 + n_delta:]
    y = DN_ALPHA * x_ref[...]
    for d in d_refs:
        y = y + d[...].astype(F32)
    mu = jnp.mean(y, axis=-1, keepdims=True)
    yc = y - mu
    var = jnp.mean(jnp.square(yc), axis=-1, keepdims=True)
    o = yc * lax.rsqrt(var + LN_EPS) * g_ref[...] + b_ref[...]
    o_ref[...] = o
    ob_ref[...] = o.astype(BF16)


def _res_ln(x, deltas, g, b, *, tm=128):
    M = x.shape[0]
    tm = min(tm, M)
    row = pl.BlockSpec((tm, D_MODEL), lambda i: (i, 0))
    vec = pl.BlockSpec((1, D_MODEL), lambda i: (0, 0))
    return pl.pallas_call(
        functools.partial(_ln_kernel, n_delta=len(deltas)),
        grid=(M // tm,),
        in_specs=[row] * (1 + len(deltas)) + [vec, vec],
        out_specs=[row, row],
        out_shape=[jax.ShapeDtypeStruct((M, D_MODEL), F32), jax.ShapeDtypeStruct((M, D_MODEL), BF16)],
        compiler_params=_cparams(("parallel",)),
        name="res_ln",
    )(x, *deltas, g, b)


OFF_A = 0
OFF_B = D_A
OFF_QC = OFF_B + 4 * D_B
OFF_KC = OFF_QC + D_C
OFF_VC = OFF_KC + D_C
OFF_FC = OFF_VC + D_C
N_GATES = 3 * D_MODEL


def _in_proj(xpb, xsb, w_in_t, layer):
    seg = lambda **kw: _proj(xpb, xsb, w_in_t, layer, transposed=True, **kw)
    parts = {
        "u_a": seg(n_off=OFF_A, n_out=D_A),
        "zb": seg(n_off=OFF_B, n_out=4 * D_B),
        "q_c": seg(n_off=OFF_QC, n_out=D_C, out_dtype=BF16),
        "k_c": seg(n_off=OFF_KC, n_out=D_C),
        "v_c": seg(n_off=OFF_VC, n_out=D_C),
        "f_c": seg(n_off=OFF_FC, n_out=LANES, tn=LANES),
        "gates": seg(n_off=OFF_FC + H_C, n_out=N_GATES, out_dtype=BF16, act="sigmoid"),
    }
    return ({k: v[0] for k, v in parts.items()}, {k: v[1] for k, v in parts.items()})


def _mixers(z, lw, *, layer, B, T, pool_buf16, pos0, hgrn_s0, lb_raw, sample=None):
    M = B * T
    tm = min(PROJ_TM, M)
    u_a, zb, q_c, k_c, v_c, f_c, gates = (z[k] for k in ("u_a", "zb", "q_c", "k_c", "v_c", "f_c", "gates"))

    y_a, nbuf = _pool_mix(u_a.reshape(B, T, D_A), pool_buf16, lw["w_pool"], lw["pool_scale"], pos0=pos0, tt=256)
    new_buf = nbuf[:, 1:]

    zb3 = zb.reshape(B, T, 4 * D_B)
    if T % HGRN_CHUNK:
        Tp = -(-T // HGRN_CHUNK) * HGRN_CHUNK
        zb3 = jnp.pad(zb3, ((0, 0), (0, Tp - T), (0, 0)))
        y_b, new_s = _hgrn2(zb3, lb_raw, hgrn_s0, lw["hgrn_norm_g"], layer=layer, tt=128, t_valid=T)
        y_b = y_b[:, :T]
    else:
        y_b, new_s = _hgrn2(zb3, lb_raw, hgrn_s0, lw["hgrn_norm_g"], layer=layer, tt=128)

    q3, k3, v3 = (a.reshape(B, T, D_C) for a in (q_c, k_c, v_c))
    if sample is None:
        logf, c = _fox_gate(f_c.reshape(B, T, LANES), lw["fox_fb"])
        y_c = _fox_prompt(q3, k3, v3, c)
    else:
        T16 = PAGE_SIZE
        f3 = jnp.pad(f_c.reshape(B, T, LANES), ((0, 0), (0, T16 - T), (0, 0)))
        logf, c = _fox_gate(f3, lw["fox_fb"], t_valid=T)
        logf = logf[:, :T]
        c_hm = jnp.transpose(c[:, :, :H_C], (0, 2, 1))
        cq_col = c_hm[:, :, :T].reshape(B, H_C * T, 1)
        ck_new = jnp.repeat(c_hm, T, axis=1)
        pad_kv = lambda a: jnp.pad(a.astype(BF16), ((0, 0), (0, T16 - T), (0, 0)))
        suffix = _fox_suffix(sample["page_table"], sample["cache_logf"][layer])
        y_c = _fox_sample(q3, pad_kv(k3), pad_kv(v3), cq_col, ck_new, suffix, sample["cache_k"], sample["cache_v"],
                          sample["page_table"], layer=layer, t_new=T)
    logf = logf[:, :, :H_C]

    merged = _merge(y_a.reshape(M, D_A), y_b.reshape(M, D_B), y_c.reshape(M, D_C),
                    lw["w_up_a"], lw["w_up_b"], lw["w_up_c"], gates, tm=tm)
    return merged, new_buf, new_s, logf


def kernel(x_prompt, x_sample, cache_k, cache_v, cache_logf, state_hgrn, state_pool, page_table, p_prompt, p_sample,
           w_in, fox_fb, w_pool, pool_scale, hgrn_lb, hgrn_norm_g, w_up_a, w_up_b, w_up_c, w_out, ln1_g, ln1_b,
           w_ff_up, w_ff_down, w_ple, w_ple_gate, ln2_g, ln2_b):
    Bp, Tp, _ = x_prompt.shape
    Bs, Ts, _ = x_sample.shape
    n_pool = cache_k.shape[1]
    past_len = page_table.shape[1] * PAGE_SIZE
    cache_logf2 = cache_logf.reshape(DEPTH, n_pool, PAGE_SIZE * H_C)
    page_rows = (DEPTH, n_pool, PAGE_SIZE * H_C, DH_C)
    sample_ctx = {"page_table": page_table, "cache_k": cache_k.reshape(page_rows),
                  "cache_v": cache_v.reshape(page_rows), "cache_logf": cache_logf2}

    xp = x_prompt.reshape(Bp * Tp, D_MODEL)
    xs = x_sample.reshape(Bs * Ts, D_MODEL)
    xpb, xsb = xp.astype(BF16), xs.astype(BF16)
    zero_buf = jnp.zeros((Bp, POOL_HALO, D_A), F32)
    zero_state = jnp.zeros((Bp, H_B, DK_B, DK_B), F32)
    lb_raw = hgrn_lb.astype(F32)
    w_in_t = jnp.swapaxes(w_in, 1, 2)
    w_down = w_ff_down.astype(BF16)

    outs = [[] for _ in range(10)]
    for i in range(DEPTH):
        lw = {
            "fox_fb": jnp.pad(fox_fb[i].astype(F32), (0, LANES - H_C)).reshape(1, LANES),
            "w_pool": w_pool[i].astype(BF16),
            "pool_scale": pool_scale[i].astype(F32).reshape(1, D_A),
            "hgrn_norm_g": hgrn_norm_g[i].astype(F32).reshape(1, DK_B),
            "w_up_a": w_up_a[i].astype(BF16), "w_up_b": w_up_b[i].astype(BF16), "w_up_c": w_up_c[i].astype(BF16),
        }
        vec = lambda a: a[i].astype(F32).reshape(1, D_MODEL)
        pp = p_prompt[i].reshape(Bp * Tp, -1).astype(BF16)
        ps = p_sample[i].reshape(Bs * Ts, -1).astype(BF16)

        zp, zs = _in_proj(xpb, xsb, w_in_t, i)
        merged_p, buf_p, s_p, f_p = _mixers(
            zp, lw, layer=i, B=Bp, T=Tp, pool_buf16=zero_buf, pos0=0, hgrn_s0=zero_state, lb_raw=lb_raw)
        buf16 = jnp.pad(state_pool[i], ((0, 0), (POOL_HALO - POOL_BUF, 0), (0, 0)))
        merged_s, buf_s, s_s, f_s = _mixers(
            zs, lw, layer=i, B=Bs, T=Ts, pool_buf16=buf16, pos0=past_len, hgrn_s0=state_hgrn[i], lb_raw=lb_raw,
            sample=sample_ctx)

        mix_p, mix_s = _proj(merged_p, merged_s, w_out, i, out_dtype=BF16)
        hp, hpb = _res_ln(xp, [mix_p], vec(ln1_g), vec(ln1_b))
        hs, hsb = _res_ln(xs, [mix_s], vec(ln1_g), vec(ln1_b))
        up_p, up_s = _proj(hpb, hsb, w_ff_up, i, out_dtype=BF16, act="relu2")
        ff_p = _mm(up_p, w_down, i)
        ff_s = _mm(up_s, w_down, i)
        ple_p, ple_s = _proj(hpb, hsb, w_ple_gate, i, act="sigmoid", side=(pp, ps, w_ple), out_dtype=BF16)
        xp, xpb = _res_ln(hp, [ff_p, ple_p], vec(ln2_g), vec(ln2_b))
        xs, xsb = _res_ln(hs, [ff_s, ple_s], vec(ln2_g), vec(ln2_b))

        vals = (zp["k_c"].reshape(Bp, Tp, H_C, DH_C), zp["v_c"].reshape(Bp, Tp, H_C, DH_C), f_p,
                zs["k_c"].reshape(Bs, Ts, H_C, DH_C), zs["v_c"].reshape(Bs, Ts, H_C, DH_C), f_s,
                s_p, s_s, buf_p, buf_s)
        for o, val in zip(outs, vals):
            o.append(val)
    return (xp.reshape(Bp, Tp, D_MODEL), xs.reshape(Bs, Ts, D_MODEL)) + tuple(jnp.stack(o) for o in outs)
```

```python
import functools
import math

import jax
import jax.numpy as jnp
from jax import lax
from jax.experimental import pallas as pl
from jax.experimental.pallas import tpu as pltpu

F32 = jnp.float32
BF16 = jnp.bfloat16

D_MODEL = 4096
DEPTH = 2
PAGE_SIZE = 128
POOL_WINDOWS = (2, 4, 8, 16)
D_A = D_MODEL // 4
G_A = D_A // len(POOL_WINDOWS)
POOL_BUF = 15
D_B = D_MODEL // 4
DK_B = 128
H_B = D_B // DK_B
D_C = D_MODEL // 2
DH_C = 128
H_C = D_C // DH_C
D_FF = 4 * D_MODEL
DN_ALPHA = (2 * DEPTH) ** 0.25
LN_EPS = 1e-5
RMS_EPS = 1e-6

LANES = 128
SUBLANES = 8
VMEM_LIMIT = 58 * 1024 * 1024

HGRN_CHUNK = 16
HGRN_ROWS = 32
POOL_HALO = 16

NEG_INF = float("-inf")


def _cparams(sem):
    return pltpu.CompilerParams(dimension_semantics=sem, vmem_limit_bytes=VMEM_LIMIT)


def _dot(a, b):
    return jnp.dot(a, b, preferred_element_type=F32)


def _dot_nt(a, b):
    return lax.dot_general(a, b, (((1,), (1,)), ((), ())), preferred_element_type=F32)


def _dot_tn(a, b):
    return lax.dot_general(a, b, (((0,), (0,)), ((), ())), preferred_element_type=F32)


def _dot01(sel, x):
    hi = x.astype(BF16)
    r1 = x - hi.astype(F32)
    mid = r1.astype(BF16)
    lo = (r1 - mid.astype(F32)).astype(BF16)
    return _dot(sel, hi) + _dot(sel, mid) + _dot(sel, lo)


def _log_sigmoid(z):
    return jnp.minimum(z, 0.0) - jnp.log1p(jnp.exp(-jnp.abs(z)))


def _sigmoid(z):
    return 0.5 * jnp.tanh(0.5 * z) + 0.5


def _mm_kernel(x_ref, w_ref, o_ref, acc_ref):
    k = pl.program_id(2)

    @pl.when(k == 0)
    def _():
        acc_ref[...] = jnp.zeros_like(acc_ref)

    acc_ref[...] += _dot(x_ref[...], w_ref[...])

    @pl.when(k == pl.num_programs(2) - 1)
    def _():
        o_ref[...] = acc_ref[...].astype(o_ref.dtype)


def _mm(x, w, layer, *, out_dtype=BF16, tm=1024, tn=1024, tk=4096):
    M, K = x.shape
    N = w.shape[2]
    tm, tn, tk = min(tm, M), min(tn, N), min(tk, K)
    assert M % tm == 0 and N % tn == 0 and K % tk == 0
    return pl.pallas_call(
        _mm_kernel,
        grid=(M // tm, N // tn, K // tk),
        in_specs=[pl.BlockSpec((tm, tk), lambda i, j, k: (i, k)),
                  pl.BlockSpec((None, tk, tn), lambda i, j, k: (layer, k, j))],
        out_specs=pl.BlockSpec((tm, tn), lambda i, j, k: (i, j)),
        out_shape=jax.ShapeDtypeStruct((M, N), out_dtype),
        scratch_shapes=[pltpu.VMEM((tm, tn), F32)],
        compiler_params=_cparams(("parallel", "parallel", "arbitrary")),
        name="dense",
    )(x, w)


PROJ_TM = 1024
PROJ_TN = 1024
CAST_ROWS = 256


def _proj_kernel(*refs, act, has_side, transposed, layer, start, tn):
    refs = list(refs)
    xp_ref, xs_ref, w_hbm = refs[:3]
    del refs[:3]
    if has_side:
        pp_ref, ps_ref, w2_ref = refs[:3]
        del refs[:3]
    op_ref, os_ref, wf32_ref, wbf_ref, sem = refs[:5]
    w2bf_ref = refs[5] if has_side else None
    j, i = pl.program_id(0), pl.program_id(1)
    nj = pl.num_programs(0)
    n_rows = wbf_ref.shape[0]
    mm = _dot_nt if transposed else _dot

    def tile_copy(jj):
        cols = pl.ds(pl.multiple_of(start + jj * tn, math.gcd(start, tn)), tn)
        src = w_hbm.at[layer, cols, :] if transposed else w_hbm.at[layer, :, cols]
        return pltpu.make_async_copy(src, wf32_ref, sem)

    def finish(acc, p_ref, o_ref):
        if act == "sigmoid":
            acc = _sigmoid(acc)
        elif act == "relu2":
            acc = jnp.square(jnp.maximum(acc, 0.0))
        if has_side:
            acc = _dot(p_ref[...], w2bf_ref[...]) * acc
        o_ref[...] = acc.astype(o_ref.dtype)

    @pl.when(i == 0)
    def _():
        @pl.when(j == 0)
        def _():
            tile_copy(0).start()

        tile_copy(j).wait()
        for r in range(0, n_rows, CAST_ROWS):
            n = min(CAST_ROWS, n_rows - r)
            wbf_ref[r:r + n, :] = wf32_ref[r:r + n, :].astype(BF16)
        if has_side:
            w2bf_ref[...] = w2_ref[...].astype(BF16)

        @pl.when(j + 1 < nj)
        def _():
            tile_copy(j + 1).start()

        finish(mm(xs_ref[...], wbf_ref[...]), ps_ref if has_side else None, os_ref)

    finish(mm(xp_ref[...], wbf_ref[...]), pp_ref if has_side else None, op_ref)


def _proj(xp, xs, w, layer, *, n_off=0, n_out=None, out_dtype=F32, act=None, side=None, transposed=False,
          tm=PROJ_TM, tn=PROJ_TN):
    Mp, K = xp.shape
    Ms = xs.shape[0]
    n_all = w.shape[1] if transposed else w.shape[2]
    n_out = n_all - n_off if n_out is None else n_out
    tm = min(tm, Mp)
    assert Mp % tm == 0 and n_out % tn == 0
    assert n_off % (SUBLANES if transposed else LANES) == 0
    nmp = Mp // tm
    row = lambda i: i
    in_specs = [pl.BlockSpec((tm, K), lambda j, i: (row(i), 0)), pl.BlockSpec((Ms, K), lambda j, i: (0, 0)),
                pl.BlockSpec(memory_space=pl.ANY)]
    args = [xp, xs, w]
    w_tile = (tn, K) if transposed else (K, tn)
    scratch = [pltpu.VMEM(w_tile, F32), pltpu.VMEM(w_tile, BF16), pltpu.SemaphoreType.DMA]
    if side is not None:
        pp, ps, w2 = side
        K2 = pp.shape[1]
        in_specs += [pl.BlockSpec((tm, K2), lambda j, i: (row(i), 0)),
                     pl.BlockSpec((Ms, K2), lambda j, i: (0, 0)),
                     pl.BlockSpec((None, K2, tn), lambda j, i: (layer, 0, j))]
        args += [pp, ps, w2]
        scratch.append(pltpu.VMEM((K2, tn), BF16))
    return pl.pallas_call(
        functools.partial(_proj_kernel, act=act, has_side=side is not None, transposed=transposed, layer=layer,
                          start=n_off, tn=tn),
        grid=(n_out // tn, nmp),
        in_specs=in_specs,
        out_specs=[pl.BlockSpec((tm, tn), lambda j, i: (row(i), j)),
                   pl.BlockSpec((Ms, tn), lambda j, i: (0, j))],
        out_shape=[jax.ShapeDtypeStruct((Mp, n_out), out_dtype), jax.ShapeDtypeStruct((Ms, n_out), out_dtype)],
        scratch_shapes=scratch,
        compiler_params=_cparams(("arbitrary", "arbitrary")),
        name="proj",
    )(*args)


def _pool_kernel(u_ref, buf_ref, w_ref, scale_ref, y_ref, nbuf_ref, ext_ref, *, tt, pos0):
    t = pl.program_id(1)

    @pl.when(t == 0)
    def _():
        ext_ref[0:POOL_HALO, :] = buf_ref[0]

    u = u_ref[0]
    ext_ref[POOL_HALO:POOL_HALO + tt, :] = u
    pos = pos0 + t * tt + lax.broadcasted_iota(jnp.int32, (tt, 1), 0)
    for gi, w in enumerate(POOL_WINDOWS):
        cols = slice(gi * G_A, (gi + 1) * G_A)
        win = u[:, cols]
        for j in range(1, w):
            win = win + ext_ref[POOL_HALO - j:POOL_HALO - j + tt, cols]
        cnt = jnp.minimum(pos + 1, w).astype(F32)
        pooled = win / cnt - u[:, cols]
        y = _dot(pooled.astype(BF16), w_ref[gi]) * scale_ref[:, cols]
        y_ref[0, :, cols] = y.astype(y_ref.dtype)
    tail = ext_ref[tt:tt + POOL_HALO, :]
    nbuf_ref[0] = tail
    ext_ref[0:POOL_HALO, :] = tail


def _pool_mix(u, buf16, w_pool, scale, *, pos0, tt):
    B, T, _ = u.shape
    tt = min(tt, T)
    assert T % tt == 0
    return pl.pallas_call(
        functools.partial(_pool_kernel, tt=tt, pos0=pos0),
        grid=(B, T // tt),
        in_specs=[pl.BlockSpec((1, tt, D_A), lambda b, t: (b, t, 0)),
                  pl.BlockSpec((1, POOL_HALO, D_A), lambda b, t: (b, 0, 0)),
                  pl.BlockSpec((len(POOL_WINDOWS), G_A, G_A), lambda b, t: (0, 0, 0)),
                  pl.BlockSpec((1, D_A), lambda b, t: (0, 0))],
        out_specs=[pl.BlockSpec((1, tt, D_A), lambda b, t: (b, t, 0)),
                   pl.BlockSpec((1, POOL_HALO, D_A), lambda b, t: (b, 0, 0))],
        out_shape=[jax.ShapeDtypeStruct((B, T, D_A), BF16),
                   jax.ShapeDtypeStruct((B, POOL_HALO, D_A), F32)],
        scratch_shapes=[pltpu.VMEM((POOL_HALO + tt, D_A), F32)],
        compiler_params=_cparams(("parallel", "arbitrary")),
        name="pool_mix",
    )(u, buf16, w_pool, scale)


def _hgrn_kernel(q_ref, f_ref, i_ref, g_ref, lbraw_ref, s0_ref, ng_ref, y_ref, sfin_ref,
                 st_ref, b_ref, kk_ref, v_ref, qd_ref, o_ref, *, layer, tt, t_valid):
    t = pl.program_id(1)
    nt = pl.num_programs(1)
    C = HGRN_CHUNK

    @pl.when(t == 0)
    def _():
        for h in range(H_B):
            st_ref[h] = s0_ref[0, h].T

    raw = lbraw_ref[...]
    e = jnp.exp(raw - jnp.max(raw, axis=0, keepdims=True))
    sm = e / jnp.sum(e, axis=0, keepdims=True)
    lb = jnp.zeros((1, D_B), F32)
    for j in range(1, layer + 1):
        lb = lb + sm[j:j + 1]
    lb = (lb + sm[0:1]) - sm[0:1]

    z = f_ref[0]
    la = jnp.log(lb)
    c = jnp.log1p(-lb) + _log_sigmoid(z)
    amax = jnp.maximum(la, c)
    delta = la - c
    lf = jnp.where(jnp.isnan(delta), la + c, amax + jnp.log1p(jnp.exp(-jnp.abs(delta))))
    kk = (1.0 - lb) * _sigmoid(-z)
    if t_valid is not None:
        row = t * tt + lax.broadcasted_iota(jnp.int32, (tt, 1), 0)
        lf = jnp.where(row < t_valid, lf, 0.0)
        kk = jnp.where(row < t_valid, kk, 0.0)
    r = lax.broadcasted_iota(jnp.int32, (tt, tt), 0)
    s = lax.broadcasted_iota(jnp.int32, (tt, tt), 1)
    tri = jnp.where((s <= r) & (s >= jnp.bitwise_and(r, -C)), 1.0, 0.0).astype(BF16)
    b_all = _dot01(tri, lf)
    qs_all = q_ref[0] * DK_B ** -0.5
    v_all = i_ref[0]
    heads = [slice(h * DK_B, (h + 1) * DK_B) for h in range(H_B)]
    halo = jnp.zeros((C, DK_B), F32)
    for h, hl in enumerate(heads):
        for ref, val in ((b_ref, b_all), (kk_ref, kk), (v_ref, v_all)):
            ref[h, 0:C, :] = halo
            ref[h, C:C + tt, :] = val[:, hl]

    nr = min(tt, HGRN_ROWS)
    tmod = jnp.bitwise_and(lax.broadcasted_iota(jnp.int32, (nr, 1), 0), C - 1)
    for h, hl in enumerate(heads):
        for r0 in range(0, tt, nr):
            rows = slice(r0, r0 + nr)
            bh, qh = b_all[rows, hl], qs_all[rows, hl]
            cur = slice(C + r0, C + r0 + nr)
            acc = jnp.sum(qh * kk_ref[h, cur, :], axis=-1, keepdims=True) * v_ref[h, cur, :]
            for d in range(1, C):
                lag = slice(C + r0 - d, C + r0 - d + nr)
                w = qh * jnp.exp(jnp.minimum(bh - b_ref[h, lag, :], 0.0)) * kk_ref[h, lag, :]
                rs = jnp.sum(w, axis=-1, keepdims=True)
                acc = acc + jnp.where(tmod >= d, rs, 0.0) * v_ref[h, lag, :]
            o_ref[rows, hl] = acc
            qd_ref[rows, hl] = qh * jnp.exp(bh)

    def chunk(ci, carry):
        rows = pl.ds(pl.multiple_of(ci * C, C), C)
        rows_h = pl.ds(pl.multiple_of(ci * C + C, C), C)
        for h, hl in enumerate(heads):
            b = b_ref[h, rows_h, :]
            v = v_ref[h, rows_h, :]
            st = st_ref[h]
            o_ref[rows, hl] += _dot_nt(qd_ref[rows, hl].astype(BF16), st.astype(BF16))
            b_last = b[C - 1:C]
            k_dec = kk_ref[h, rows_h, :] * jnp.exp(b_last - b)
            st_ref[h] = st * jnp.exp(b_last) + _dot_tn(v.astype(BF16), k_dec.astype(BF16))
        return carry

    n_chunks = tt // C
    lax.fori_loop(0, n_chunks, chunk, 0, unroll=math.gcd(n_chunks, 8))

    for h, hl in enumerate(heads):
        o = o_ref[:, hl]
        o = o * lax.rsqrt(jnp.mean(jnp.square(o), axis=-1, keepdims=True) + RMS_EPS) * ng_ref[...]
        o = o * _sigmoid(g_ref[0, :, hl])
        y_ref[0, :, hl] = o.astype(y_ref.dtype)

    @pl.when(t == nt - 1)
    def _():
        for h in range(H_B):
            sfin_ref[0, h] = st_ref[h].T


def _hgrn2(zb, lb_raw, s0, norm_g, *, layer, tt, t_valid=None):
    B, T, _ = zb.shape
    tt = min(tt, T)
    assert T % tt == 0 and tt % HGRN_CHUNK == 0

    def col(j):
        return pl.BlockSpec((1, tt, D_B), lambda b, t: (b, t, j))

    return pl.pallas_call(
        functools.partial(_hgrn_kernel, layer=layer, tt=tt, t_valid=t_valid),
        grid=(B, T // tt),
        in_specs=[col(0), col(1), col(2), col(3),
                  pl.BlockSpec((DEPTH, D_B), lambda b, t: (0, 0)),
                  pl.BlockSpec((1, H_B, DK_B, DK_B), lambda b, t: (b, 0, 0, 0)),
                  pl.BlockSpec((1, DK_B), lambda b, t: (0, 0))],
        out_specs=[pl.BlockSpec((1, tt, D_B), lambda b, t: (b, t, 0)),
                   pl.BlockSpec((1, H_B, DK_B, DK_B), lambda b, t: (b, 0, 0, 0))],
        out_shape=[jax.ShapeDtypeStruct((B, T, D_B), BF16),
                   jax.ShapeDtypeStruct((B, H_B, DK_B, DK_B), F32)],
        scratch_shapes=[pltpu.VMEM((H_B, DK_B, DK_B), F32)]
        + [pltpu.VMEM((H_B, HGRN_CHUNK + tt, DK_B), F32)] * 3
        + [pltpu.VMEM((tt, D_B), F32)] * 2,
        compiler_params=_cparams(("parallel", "arbitrary")),
        name="hgrn2",
    )(zb, zb, zb, zb, lb_raw, s0, norm_g)


def _fox_gate_kernel(f_ref, fb_ref, logf_ref, c_ref, *, t_valid):
    S = f_ref.shape[1]
    logf = _log_sigmoid(f_ref[0] + fb_ref[...])
    logf_ref[0] = logf
    if t_valid is not None:
        row = lax.broadcasted_iota(jnp.int32, (S, 1), 0)
        logf = jnp.where(row < t_valid, logf, 0.0)
    r = lax.broadcasted_iota(jnp.int32, (S, S), 0)
    s = lax.broadcasted_iota(jnp.int32, (S, S), 1)
    tri = jnp.where(s <= r, 1.0, 0.0).astype(BF16)
    c_ref[0] = _dot01(tri, logf)


def _fox_gate(f_pad, fb_pad, *, t_valid=None):
    B, S, L = f_pad.shape
    spec = pl.BlockSpec((1, S, L), lambda b: (b, 0, 0))
    return pl.pallas_call(
        functools.partial(_fox_gate_kernel, t_valid=t_valid),
        grid=(B,),
        in_specs=[spec, pl.BlockSpec((1, L), lambda b: (0, 0))],
        out_specs=[spec, spec],
        out_shape=[jax.ShapeDtypeStruct((B, S, L), F32)] * 2,
        compiler_params=_cparams(("parallel",)),
        name="fox_gate",
    )(f_pad, fb_pad)


def _split3(c):
    hi = c.astype(BF16).astype(F32)
    r1 = c - hi
    mid = r1.astype(BF16).astype(F32)
    lo = (r1 - mid).astype(BF16).astype(F32)
    return hi, mid, lo


def _bias_lanes(c, sign, first):
    hi, mid, lo = _split3(c)
    lane = lax.broadcasted_iota(jnp.int32, (c.shape[0], LANES), 1)
    base = 0 if first else 3
    vals = jnp.where(lane == base, hi, jnp.where(lane == base + 1, mid, jnp.where(lane == base + 2, lo, 0.0))) * sign
    ones = jnp.where((lane >= 3 - base) & (lane < 6 - base), 1.0, 0.0)
    return (vals + ones).astype(BF16)


def _fox_prompt_kernel(q_ref, k_ref, v_ref, c_ref, o_ref, kaug_ref, vb_ref, *, tq):
    h = pl.program_id(1)
    qi = pl.program_id(2)
    S = k_ref.shape[1]
    scale = DH_C ** -0.5
    lane = lax.broadcasted_iota(jnp.int32, (tq, LANES), 1)

    def head_col(rows):
        return jnp.sum(jnp.where(lane == h, c_ref[0, rows, :], 0.0), axis=-1, keepdims=True)

    @pl.when(qi == 0)
    def _():
        for r in range(0, S, tq):
            rows = slice(r, r + tq)
            kaug_ref[rows, 0:DH_C] = (k_ref[0, rows, :] * scale).astype(BF16)
            kaug_ref[rows, DH_C:] = _bias_lanes(head_col(rows), -1.0, first=False)
            vb_ref[rows, :] = v_ref[0, rows, :].astype(BF16)

    q_rows = pl.ds(pl.multiple_of(qi * tq, tq), tq)
    q = jnp.concatenate([q_ref[0], _bias_lanes(head_col(q_rows), 1.0, first=True)], axis=1)

    def block(j, carry, masked):
        m, l, acc = carry
        ks = slice(j * tq, (j + 1) * tq)
        s = _dot_nt(q, kaug_ref[ks, :])
        if masked:
            rr = lax.broadcasted_iota(jnp.int32, (tq, tq), 0)
            cc = lax.broadcasted_iota(jnp.int32, (tq, tq), 1)
            s = jnp.where(cc <= rr, s, NEG_INF)
        m_new = jnp.maximum(m, jnp.max(s, axis=-1, keepdims=True))
        alpha = jnp.exp(m - m_new)
        p = jnp.exp(s - m_new)
        l = alpha * l + jnp.sum(p, axis=-1, keepdims=True)
        acc = alpha * acc + _dot(p.astype(BF16), vb_ref[ks, :])
        return m_new, l, acc

    init = (jnp.full((tq, 1), NEG_INF, F32), jnp.zeros((tq, 1), F32), jnp.zeros((tq, DH_C), F32))
    for n in range(S // tq):

        @pl.when(qi == n)
        def _(n=n):
            carry = block(n, init, True)
            for j in range(n):
                carry = block(j, carry, False)
            m, l, acc = carry
            o_ref[0] = (acc / l).astype(o_ref.dtype)


def _fox_prompt(q, k, v, c, *, tq=512):
    B, S, _ = q.shape
    tq = min(tq, S)
    assert S % tq == 0
    kv_spec = pl.BlockSpec((1, S, DH_C), lambda b, h, i: (b, 0, h))
    return pl.pallas_call(
        functools.partial(_fox_prompt_kernel, tq=tq),
        grid=(B, H_C, S // tq),
        in_specs=[pl.BlockSpec((1, tq, DH_C), lambda b, h, i: (b, i, h)), kv_spec, kv_spec,
                  pl.BlockSpec((1, S, LANES), lambda b, h, i: (b, 0, 0))],
        out_specs=pl.BlockSpec((1, tq, DH_C), lambda b, h, i: (b, i, h)),
        out_shape=jax.ShapeDtypeStruct((B, S, D_C), BF16),
        scratch_shapes=[pltpu.VMEM((S, 2 * DH_C), BF16), pltpu.VMEM((S, DH_C), BF16)],
        compiler_params=_cparams(("parallel", "parallel", "arbitrary")),
        name="fox_prompt",
    )(q, k, v, c)


def _fox_suffix_kernel(pt_ref, lf_ref, usel_ref, tsel_ref, esel_ref, o_ref, g_ref, *, n_pages):
    b = pl.program_id(0)
    for p in range(n_pages):
        g_ref[p:p + 1, :] = lf_ref[pl.ds(pt_ref[b, p], 1), :]
    g = g_ref[...]
    within = _dot01_rhs(g, usel_ref[...])
    tot = _dot01_rhs(g, tsel_ref[...])
    r = lax.broadcasted_iota(jnp.int32, (n_pages, n_pages), 0)
    s = lax.broadcasted_iota(jnp.int32, (n_pages, n_pages), 1)
    later = jnp.where(s > r, 1.0, 0.0).astype(BF16)
    carry = _dot01(later, tot)
    suf = within + _dot01_rhs(carry, esel_ref[...])
    for h in range(H_C):
        o_ref[0, h] = suf[:, h * PAGE_SIZE:(h + 1) * PAGE_SIZE]


def _dot01_rhs(x, sel):
    hi = x.astype(BF16)
    r1 = x - hi.astype(F32)
    mid = r1.astype(BF16)
    lo = (r1 - mid.astype(F32)).astype(BF16)
    return _dot(hi, sel) + _dot(mid, sel) + _dot(lo, sel)


def _fox_suffix(page_table, cache_logf_l):
    B, n_pages = page_table.shape
    n_pool, W = cache_logf_l.shape
    key_in = jnp.arange(W) // H_C
    head_in = jnp.arange(W) % H_C
    head_out = jnp.arange(W) // PAGE_SIZE
    key_out = jnp.arange(W) % PAGE_SIZE
    usel = ((head_in[:, None] == head_out[None, :]) & (key_in[:, None] > key_out[None, :])).astype(BF16)
    tsel = (head_in[:, None] == jnp.arange(LANES)[None, :]).astype(BF16)
    esel = (jnp.arange(LANES)[:, None] == head_out[None, :]).astype(BF16)
    full = lambda shp: pl.BlockSpec(shp, lambda b, pt: tuple(0 for _ in shp))
    grid_spec = pltpu.PrefetchScalarGridSpec(
        num_scalar_prefetch=1,
        grid=(B,),
        in_specs=[full((n_pool, W)), full((W, W)), full((W, LANES)), full((LANES, W))],
        out_specs=pl.BlockSpec((1, H_C, n_pages, PAGE_SIZE), lambda b, pt: (b, 0, 0, 0)),
        scratch_shapes=[pltpu.VMEM((n_pages, W), F32)],
    )
    return pl.pallas_call(
        functools.partial(_fox_suffix_kernel, n_pages=n_pages),
        grid_spec=grid_spec,
        out_shape=jax.ShapeDtypeStruct((B, H_C, n_pages, PAGE_SIZE), F32),
        compiler_params=_cparams(("arbitrary",)),
        name="fox_suffix",
    )(page_table, cache_logf_l, usel, tsel, esel)


def _fox_sample_kernel(pt_ref, q_ref, kn_ref, vn_ref, cq_ref, ckn_ref, suf_ref, *rest, t_new, ppg):
    k_refs = rest[:ppg]
    v_refs = rest[ppg:2 * ppg]
    o_ref, m_ref, l_ref, acc_ref = rest[2 * ppg:]
    g = pl.program_id(1)
    ng = pl.num_programs(1)
    scale = DH_C ** -0.5
    R = H_C * t_new

    def q_head(h):
        return q_ref[0, :, h * DH_C:(h + 1) * DH_C]

    def update(s, v_of_head):
        m = m_ref[...]
        m_new = jnp.maximum(m, jnp.max(s, axis=-1, keepdims=True))
        alpha = jnp.exp(m - m_new)
        p = jnp.exp(s - m_new)
        l_ref[...] = alpha * l_ref[...] + jnp.sum(p, axis=-1, keepdims=True)
        m_ref[...] = m_new
        for h in range(H_C):
            rows = slice(h * t_new, (h + 1) * t_new)
            acc_ref[rows, :] = alpha[rows] * acc_ref[rows, :] + _dot(p[rows].astype(BF16), v_of_head(h))

    @pl.when(g == 0)
    def _():
        m_ref[...] = jnp.full_like(m_ref, NEG_INF)
        l_ref[...] = jnp.zeros_like(l_ref)
        acc_ref[...] = jnp.zeros_like(acc_ref)
        s = jnp.concatenate(
            [_dot_nt(q_head(h), kn_ref[0, :, h * DH_C:(h + 1) * DH_C]) for h in range(H_C)], axis=0) * scale
        s = s + (cq_ref[0] - ckn_ref[0])
        qpos = lax.broadcasted_iota(jnp.int32, s.shape, 0) % t_new
        kpos = lax.broadcasted_iota(jnp.int32, s.shape, 1)
        s = jnp.where(kpos <= qpos, s, NEG_INF)
        update(s, lambda h: vn_ref[0, :, h * DH_C:(h + 1) * DH_C])

    def head_rows(ref, h):
        return ref[0, 0, pl.ds(h, PAGE_SIZE, stride=H_C), :].astype(BF16)

    for j in range(ppg):
        kj, vj = k_refs[j], v_refs[j]
        s = jnp.concatenate([_dot_nt(q_head(h), head_rows(kj, h)) for h in range(H_C)], axis=0) * scale
        bias = jnp.concatenate(
            [jnp.broadcast_to(suf_ref[0, h, pl.ds(g * ppg + j, 1), :], (t_new, PAGE_SIZE)) for h in range(H_C)], axis=0)
        s = s + (cq_ref[0] + bias)
        update(s, lambda h: head_rows(vj, h))

    @pl.when(g == ng - 1)
    def _():
        o = acc_ref[...] / l_ref[...]
        for h in range(H_C):
            o_ref[0, :, h * DH_C:(h + 1) * DH_C] = o[h * t_new:(h + 1) * t_new].astype(o_ref.dtype)


def _fox_sample(q, k_new, v_new, cq_col, ck_new, suffix, cache_k, cache_v, page_table, *, layer, t_new, ppg=8):
    B, n_pages = page_table.shape
    T = q.shape[1]
    T16 = k_new.shape[1]
    R = H_C * t_new
    assert n_pages % ppg == 0

    def page_spec(j):
        return pl.BlockSpec((1, 1, PAGE_SIZE * H_C, DH_C),
                            lambda b, g, pt: (layer, pt[b, g * ppg + j], 0, 0))

    bspec = lambda shp: pl.BlockSpec((1,) + shp, lambda b, g, pt: (b,) + tuple(0 for _ in shp))
    grid_spec = pltpu.PrefetchScalarGridSpec(
        num_scalar_prefetch=1,
        grid=(B, n_pages // ppg),
        in_specs=[bspec((T, D_C)), bspec((T16, D_C)), bspec((T16, D_C)), bspec((R, 1)), bspec((R, T16)),
                  bspec((H_C, n_pages, PAGE_SIZE))]
        + [page_spec(j) for j in range(ppg)] + [page_spec(j) for j in range(ppg)],
        out_specs=bspec((T, D_C)),
        scratch_shapes=[pltpu.VMEM((R, 1), F32), pltpu.VMEM((R, 1), F32), pltpu.VMEM((R, DH_C), F32)],
    )
    return pl.pallas_call(
        functools.partial(_fox_sample_kernel, t_new=t_new, ppg=ppg),
        grid_spec=grid_spec,
        out_shape=jax.ShapeDtypeStruct((B, T, D_C), BF16),
        compiler_params=_cparams(("parallel", "arbitrary")),
        name="fox_sample",
    )(page_table, q, k_new, v_new, cq_col, ck_new, suffix, *([cache_k] * ppg), *([cache_v] * ppg))


def _merge_kernel(ya_ref, yb_ref, yc_ref, wa_ref, wb_ref, wc_ref, ga_ref, gb_ref, gc_ref, o_ref):
    m = ga_ref[...].astype(F32) * _dot(ya_ref[...], wa_ref[...])
    m = m + gb_ref[...].astype(F32) * _dot(yb_ref[...], wb_ref[...])
    m = m + gc_ref[...].astype(F32) * _dot(yc_ref[...], wc_ref[...])
    o_ref[...] = m.astype(o_ref.dtype)


def _merge(y_a, y_b, y_c, w_a, w_b, w_c, gates, *, tm=1024, tn=1024):
    M = y_a.shape[0]
    tm = min(tm, M)
    nj = D_MODEL // tn
    act = lambda K: pl.BlockSpec((tm, K), lambda i, j: (i, 0))
    wsp = lambda K: pl.BlockSpec((K, tn), lambda i, j: (0, j))
    gsp = lambda o: pl.BlockSpec((tm, tn), lambda i, j: (i, j + o * nj))
    return pl.pallas_call(
        _merge_kernel,
        grid=(M // tm, nj),
        in_specs=[act(D_A), act(D_B), act(D_C), wsp(D_A), wsp(D_B), wsp(D_C), gsp(0), gsp(1), gsp(2)],
        out_specs=pl.BlockSpec((tm, tn), lambda i, j: (i, j)),
        out_shape=jax.ShapeDtypeStruct((M, D_MODEL), BF16),
        compiler_params=_cparams(("parallel", "parallel")),
        name="merge",
    )(y_a, y_b, y_c, w_a, w_b, w_c, gates, gates, gates)


def _ln_kernel(*refs, n_delta):
    x_ref = refs[0]
    d_refs = refs[1:1 + n_delta]
    g_ref, b_ref, o_ref, ob_ref = refs[1 + n_delta:]
    y = DN_ALPHA * x_ref[...]
    for d in d_refs:
        y = y + d[...].astype(F32)
    mu = jnp.mean(y, axis=-1, keepdims=True)
    yc = y - mu
    var = jnp.mean(jnp.square(yc), axis=-1, keepdims=True)
    o = yc * lax.rsqrt(var + LN_EPS) * g_ref[...] + b_ref[...]
    o_ref[...] = o
    ob_ref[...] = o.astype(BF16)


def _res_ln(x, deltas, g, b, *, tm=128):
    M = x.shape[0]
    tm = min(tm, M)
    row = pl.BlockSpec((tm, D_MODEL), lambda i: (i, 0))
    vec = pl.BlockSpec((1, D_MODEL), lambda i: (0, 0))
    return pl.pallas_call(
        functools.partial(_ln_kernel, n_delta=len(deltas)),
        grid=(M // tm,),
        in_specs=[row] * (1 + len(deltas)) + [vec, vec],
        out_specs=[row, row],
        out_shape=[jax.ShapeDtypeStruct((M, D_MODEL), F32), jax.ShapeDtypeStruct((M, D_MODEL), BF16)],
        compiler_params=_cparams(("parallel",)),
        name="res_ln",
    )(x, *deltas, g, b)


OFF_A = 0
OFF_B = D_A
OFF_QC = OFF_B + 4 * D_B
OFF_KC = OFF_QC + D_C
OFF_VC = OFF_KC + D_C
OFF_FC = OFF_VC + D_C
N_GATES = 3 * D_MODEL


def _in_proj(xpb, xsb, w_in_t, layer):
    seg = lambda **kw: _proj(xpb, xsb, w_in_t, layer, transposed=True, **kw)
    parts = {
        "u_a": seg(n_off=OFF_A, n_out=D_A),
        "zb": seg(n_off=OFF_B, n_out=4 * D_B),
        "q_c": seg(n_off=OFF_QC, n_out=D_C, out_dtype=BF16),
        "k_c": seg(n_off=OFF_KC, n_out=D_C),
        "v_c": seg(n_off=OFF_VC, n_out=D_C),
        "f_c": seg(n_off=OFF_FC, n_out=LANES, tn=LANES),
        "gates": seg(n_off=OFF_FC + H_C, n_out=N_GATES, out_dtype=BF16, act="sigmoid"),
    }
    return ({k: v[0] for k, v in parts.items()}, {k: v[1] for k, v in parts.items()})


def _mixers(z, lw, *, layer, B, T, pool_buf16, pos0, hgrn_s0, lb_raw, sample=None):
    M = B * T
    tm = min(PROJ_TM, M)
    u_a, zb, q_c, k_c, v_c, f_c, gates = (z[k] for k in ("u_a", "zb", "q_c", "k_c", "v_c", "f_c", "gates"))

    y_a, nbuf = _pool_mix(u_a.reshape(B, T, D_A), pool_buf16, lw["w_pool"], lw["pool_scale"], pos0=pos0, tt=256)
    new_buf = nbuf[:, 1:]

    zb3 = zb.reshape(B, T, 4 * D_B)
    if T % HGRN_CHUNK:
        Tp = -(-T // HGRN_CHUNK) * HGRN_CHUNK
        zb3 = jnp.pad(zb3, ((0, 0), (0, Tp - T), (0, 0)))
        y_b, new_s = _hgrn2(zb3, lb_raw, hgrn_s0, lw["hgrn_norm_g"], layer=layer, tt=128, t_valid=T)
        y_b = y_b[:, :T]
    else:
        y_b, new_s = _hgrn2(zb3, lb_raw, hgrn_s0, lw["hgrn_norm_g"], layer=layer, tt=128)

    q3, k3, v3 = (a.reshape(B, T, D_C) for a in (q_c, k_c, v_c))
    if sample is None:
        logf, c = _fox_gate(f_c.reshape(B, T, LANES), lw["fox_fb"])
        y_c = _fox_prompt(q3, k3, v3, c)
    else:
        T16 = PAGE_SIZE
        f3 = jnp.pad(f_c.reshape(B, T, LANES), ((0, 0), (0, T16 - T), (0, 0)))
        logf, c = _fox_gate(f3, lw["fox_fb"], t_valid=T)
        logf = logf[:, :T]
        c_hm = jnp.transpose(c[:, :, :H_C], (0, 2, 1))
        cq_col = c_hm[:, :, :T].reshape(B, H_C * T, 1)
        ck_new = jnp.repeat(c_hm, T, axis=1)
        pad_kv = lambda a: jnp.pad(a.astype(BF16), ((0, 0), (0, T16 - T), (0, 0)))
        suffix = _fox_suffix(sample["page_table"], sample["cache_logf"][layer])
        y_c = _fox_sample(q3, pad_kv(k3), pad_kv(v3), cq_col, ck_new, suffix, sample["cache_k"], sample["cache_v"],
                          sample["page_table"], layer=layer, t_new=T)
    logf = logf[:, :, :H_C]

    merged = _merge(y_a.reshape(M, D_A), y_b.reshape(M, D_B), y_c.reshape(M, D_C),
                    lw["w_up_a"], lw["w_up_b"], lw["w_up_c"], gates, tm=tm)
    return merged, new_buf, new_s, logf


def kernel(x_prompt, x_sample, cache_k, cache_v, cache_logf, state_hgrn, state_pool, page_table, p_prompt, p_sample,
           w_in, fox_fb, w_pool, pool_scale, hgrn_lb, hgrn_norm_g, w_up_a, w_up_b, w_up_c, w_out, ln1_g, ln1_b,
           w_ff_up, w_ff_down, w_ple, w_ple_gate, ln2_g, ln2_b):
    Bp, Tp, _ = x_prompt.shape
    Bs, Ts, _ = x_sample.shape
    n_pool = cache_k.shape[1]
    past_len = page_table.shape[1] * PAGE_SIZE
    cache_logf2 = cache_logf.reshape(DEPTH, n_pool, PAGE_SIZE * H_C)
    page_rows = (DEPTH, n_pool, PAGE_SIZE * H_C, DH_C)
    sample_ctx = {"page_table": page_table, "cache_k": cache_k.reshape(page_rows),
                  "cache_v": cache_v.reshape(page_rows), "cache_logf": cache_logf2}

    xp = x_prompt.reshape(Bp * Tp, D_MODEL)
    xs = x_sample.reshape(Bs * Ts, D_MODEL)
    xpb, xsb = xp.astype(BF16), xs.astype(BF16)
    zero_buf = jnp.zeros((Bp, POOL_HALO, D_A), F32)
    zero_state = jnp.zeros((Bp, H_B, DK_B, DK_B), F32)
    lb_raw = hgrn_lb.astype(F32)
    w_in_t = jnp.swapaxes(w_in, 1, 2)
    w_down = w_ff_down.astype(BF16)

    outs = [[] for _ in range(10)]
    for i in range(DEPTH):
        lw = {
            "fox_fb": jnp.pad(fox_fb[i].astype(F32), (0, LANES - H_C)).reshape(1, LANES),
            "w_pool": w_pool[i].astype(BF16),
            "pool_scale": pool_scale[i].astype(F32).reshape(1, D_A),
            "hgrn_norm_g": hgrn_norm_g[i].astype(F32).reshape(1, DK_B),
            "w_up_a": w_up_a[i].astype(BF16), "w_up_b": w_up_b[i].astype(BF16), "w_up_c": w_up_c[i].astype(BF16),
        }
        vec = lambda a: a[i].astype(F32).reshape(1, D_MODEL)
        pp = p_prompt[i].reshape(Bp * Tp, -1).astype(BF16)
        ps = p_sample[i].reshape(Bs * Ts, -1).astype(BF16)

        zp, zs = _in_proj(xpb, xsb, w_in_t, i)
        merged_p, buf_p, s_p, f_p = _mixers(
            zp, lw, layer=i, B=Bp, T=Tp, pool_buf16=zero_buf, pos0=0, hgrn_s0=zero_state, lb_raw=lb_raw)
        buf16 = jnp.pad(state_pool[i], ((0, 0), (POOL_HALO - POOL_BUF, 0), (0, 0)))
        merged_s, buf_s, s_s, f_s = _mixers(
            zs, lw, layer=i, B=Bs, T=Ts, pool_buf16=buf16, pos0=past_len, hgrn_s0=state_hgrn[i], lb_raw=lb_raw,
            sample=sample_ctx)

        mix_p, mix_s = _proj(merged_p, merged_s, w_out, i, out_dtype=BF16)
        hp, hpb = _res_ln(xp, [mix_p], vec(ln1_g), vec(ln1_b))
        hs, hsb = _res_ln(xs, [mix_s], vec(ln1_g), vec(ln1_b))
        up_p, up_s = _proj(hpb, hsb, w_ff_up, i, out_dtype=BF16, act="relu2")
        ff_p = _mm(up_p, w_down, i)
        ff_s = _mm(up_s, w_down, i)
        ple_p, ple_s = _proj(hpb, hsb, w_ple_gate, i, act="sigmoid", side=(pp, ps, w_ple), out_dtype=BF16,
                                   tm=PROJ_TM // 2)
        xp, xpb = _res_ln(hp, [ff_p, ple_p], vec(ln2_g), vec(ln2_b))
        xs, xsb = _res_ln(hs, [ff_s, ple_s], vec(ln2_g), vec(ln2_b))

        vals = (zp["k_c"].reshape(Bp, Tp, H_C, DH_C), zp["v_c"].reshape(Bp, Tp, H_C, DH_C), f_p,
                zs["k_c"].reshape(Bs, Ts, H_C, DH_C), zs["v_c"].reshape(Bs, Ts, H_C, DH_C), f_s,
                s_p, s_s, buf_p, buf_s)
        for o, val in zip(outs, vals):
            o.append(val)
    return (xp.reshape(Bp, Tp, D_MODEL), xs.reshape(Bs, Ts, D_MODEL)) + tuple(jnp.stack(o) for o in outs)
```
